```python
import jax, jax.numpy as jnp
from jax import lax
import numpy as np

D_MODEL = 2048
BATCH = 4
SEQ = 2048
DEPTH = 1

CHUNK = 128
GMLP_GROUPS = 8
GMLP_GROUP_DIM = 128
GMLP_WIDTH = GMLP_GROUPS * GMLP_GROUP_DIM
ATTN_HEADS = 8
HEAD_DIM = 128
ATTN_WIDTH = ATTN_HEADS * HEAD_DIM
Q_BLOCK = 128
D_FF = 5632
CONV_WIDTH = 3
PLE_DIM = 256
N_BRANCH = 2
EPS = 1e-6
IN_COLS = 2 * GMLP_WIDTH + 3 * ATTN_WIDTH + ATTN_HEADS + N_BRANCH * D_MODEL

kernel_name = "hybrid_gmlp_fox_convffn_ple_block"


def _rms_norm(x, g):
    xf = x.astype(jnp.float32)
    y = xf * lax.rsqrt(jnp.mean(xf * xf, axis=-1, keepdims=True) + EPS)
    return (y * g.astype(jnp.float32)).astype(x.dtype)


def _layer_norm(x, g, b):
    xf = x.astype(jnp.float32)
    mu = jnp.mean(xf, axis=-1, keepdims=True)
    xc = xf - mu
    y = xc * lax.rsqrt(jnp.mean(xc * xc, axis=-1, keepdims=True) + EPS)
    return (y * g.astype(jnp.float32) + b.astype(jnp.float32)).astype(x.dtype)


def _spatial_gating(u, v, ln_g, ln_b, w_s, b_s):
    B, S, _ = v.shape
    n_chunks = S // CHUNK
    v = _layer_norm(v, ln_g, ln_b)
    vc = v.reshape(B, n_chunks, CHUNK, GMLP_GROUPS, GMLP_GROUP_DIM)
    causal = jnp.tril(jnp.ones((CHUNK, CHUNK), dtype=bool))
    w = jnp.where(causal[None], w_s, jnp.zeros((), w_s.dtype))
    mixed = jnp.einsum('gts,bcsgd->bctgd', w, vc) + b_s.T[None, None, :, :, None]
    return u * mixed.reshape(B, S, GMLP_WIDTH)


def _forgetting_attention(q, k, v, f_logit, q_norm_g, k_norm_g):
    B, S = q.shape[0], q.shape[1]
    q = _rms_norm(q, q_norm_g).transpose(0, 2, 1, 3)
    k = _rms_norm(k, k_norm_g).transpose(0, 2, 1, 3)
    v = v.transpose(0, 2, 1, 3)
    log_f = jax.nn.log_sigmoid(f_logit.astype(jnp.float32))
    cum = jnp.cumsum(log_f, axis=1).transpose(0, 2, 1)
    scale = HEAD_DIM ** -0.5
    outs = []
    for blk in range(S // Q_BLOCK):
        q0 = blk * Q_BLOCK
        q1 = q0 + Q_BLOCK
        qb = q[:, :, q0:q1]
        kb = k[:, :, :q1]
        vb = v[:, :, :q1]
        s = jnp.einsum('bhqd,bhkd->bhqk', qb, kb).astype(jnp.float32) * scale
        s = s + cum[:, :, q0:q1, None] - cum[:, :, None, :q1]
        causal = jnp.arange(q0, q1)[:, None] >= jnp.arange(q1)[None, :]
        s = jnp.where(causal, s, -jnp.inf)
        pr = jax.nn.softmax(s, axis=-1).astype(v.dtype)
        outs.append(jnp.einsum('bhqk,bhkd->bhqd', pr, vb))
    o = jnp.concatenate(outs, axis=2)
    return o.transpose(0, 2, 1, 3).reshape(B, S, ATTN_WIDTH)


def _conv_ffn(h, w_up, conv_w, conv_b, w_down):
    S = h.shape[1]
    up = h @ w_up
    a, b = jnp.split(up, 2, axis=-1)
    ap = jnp.pad(a, ((0, 0), (CONV_WIDTH - 1, 0), (0, 0)))
    conv = conv_b
    for j in range(CONV_WIDTH):
        conv = conv + ap[:, j:j + S] * conv_w[j]
    return (jax.nn.gelu(conv, approximate=True) * b) @ w_down


def setup_inputs(seed: int = 0) -> dict:
    key = jax.random.key(seed)
    ks = jax.random.split(key, 24)
    f32 = jnp.float32
    n = lambda k, shape, s: jax.random.normal(k, shape, f32) * s
    gain = lambda k, shape: 1.0 + 0.05 * jax.random.normal(k, shape, f32)
    return {
        "x": jax.random.normal(ks[0], (BATCH, SEQ, D_MODEL), f32),
        "p": jax.random.normal(ks[1], (DEPTH, BATCH, SEQ, PLE_DIM), f32),
        "norm_mix_g": gain(ks[2], (DEPTH, D_MODEL)),
        "w_in": n(ks[3], (DEPTH, D_MODEL, IN_COLS), D_MODEL ** -0.5),
        "gmlp_ln_g": gain(ks[4], (DEPTH, GMLP_WIDTH)),
        "gmlp_ln_b": n(ks[5], (DEPTH, GMLP_WIDTH), 0.02),
        "gmlp_w_s": n(ks[6], (DEPTH, GMLP_GROUPS, CHUNK, CHUNK), 0.5 * CHUNK ** -0.5),
        "gmlp_b_s": 1.0 + n(ks[7], (DEPTH, GMLP_GROUPS, CHUNK), 0.1),
        "fox_b_f": jax.random.uniform(ks[8], (DEPTH, ATTN_HEADS), f32, 1.0, 4.0),
        "q_norm_g": gain(ks[9], (DEPTH, HEAD_DIM)),
        "k_norm_g": gain(ks[10], (DEPTH, HEAD_DIM)),
        "w_branch_a": n(ks[11], (DEPTH, GMLP_WIDTH, D_MODEL), GMLP_WIDTH ** -0.5),
        "w_branch_b": n(ks[12], (DEPTH, ATTN_WIDTH, D_MODEL), ATTN_WIDTH ** -0.5),
        "w_out": n(ks[13], (DEPTH, D_MODEL, D_MODEL), D_MODEL ** -0.5),
        "norm_ffn_g": gain(ks[14], (DEPTH, D_MODEL)),
        "w_up": n(ks[15], (DEPTH, D_MODEL, 2 * D_FF), D_MODEL ** -0.5),
        "conv_w": n(ks[16], (DEPTH, CONV_WIDTH, D_FF), CONV_WIDTH ** -0.5),
        "conv_b": n(ks[17], (DEPTH, D_FF), 0.02),
        "w_down": n(ks[18], (DEPTH, D_FF, D_MODEL), D_FF ** -0.5),
        "ple_proj": n(ks[19], (DEPTH, PLE_DIM, D_MODEL), PLE_DIM ** -0.5),
        "ple_norm_g": gain(ks[20], (DEPTH, D_MODEL)),
        "ple_gate_norm_g": gain(ks[21], (DEPTH, D_MODEL)),
        "w_ple_gate": n(ks[22], (DEPTH, D_MODEL, D_MODEL), D_MODEL ** -0.5),
    }


def reference(x, p, norm_mix_g, w_in, gmlp_ln_g, gmlp_ln_b, gmlp_w_s, gmlp_b_s, fox_b_f,
              q_norm_g, k_norm_g, w_branch_a, w_branch_b, w_out, norm_ffn_g, w_up,
              conv_w, conv_b, w_down, ple_proj, ple_norm_g, ple_gate_norm_g, w_ple_gate):
    B, S, _ = x.shape
    sizes = [GMLP_WIDTH, GMLP_WIDTH, ATTN_WIDTH, ATTN_WIDTH, ATTN_WIDTH, ATTN_HEADS,
             D_MODEL, D_MODEL]
    points = []
    acc = 0
    for sz in sizes[:-1]:
        acc += sz
        points.append(acc)
    for i in range(DEPTH):
        h = _rms_norm(x, norm_mix_g[i])
        z = h @ w_in[i]
        u, v, q, k, vv, f, g_a, g_b = jnp.split(z, points, axis=-1)
        o_a = _spatial_gating(jax.nn.gelu(u, approximate=False), jax.nn.gelu(v, approximate=False),
                              gmlp_ln_g[i], gmlp_ln_b[i], gmlp_w_s[i], gmlp_b_s[i])
        o_b = _forgetting_attention(q.reshape(B, S, ATTN_HEADS, HEAD_DIM),
                                    k.reshape(B, S, ATTN_HEADS, HEAD_DIM),
                                    vv.reshape(B, S, ATTN_HEADS, HEAD_DIM),
                                    f + fox_b_f[i], q_norm_g[i], k_norm_g[i])
        y = jax.nn.sigmoid(g_a) * (o_a @ w_branch_a[i]) + jax.nn.sigmoid(g_b) * (o_b @ w_branch_b[i])
        x = x + y @ w_out[i]
        x = x + _conv_ffn(_rms_norm(x, norm_ffn_g[i]), w_up[i], conv_w[i], conv_b[i], w_down[i])
        e = _rms_norm(p[i] @ ple_proj[i], ple_norm_g[i])
        gate = jax.nn.sigmoid(_rms_norm(x, ple_gate_norm_g[i]) @ w_ple_gate[i])
        x = x + gate * e
    return x
```

```python
import functools

import jax
import jax.numpy as jnp
from jax import lax
from jax.experimental import pallas as pl
from jax.experimental.pallas import tpu as pltpu

D_MODEL = 2048
CHUNK = 128
GROUPS = 8
GROUP_DIM = 128
GMLP_WIDTH = GROUPS * GROUP_DIM
HEADS = 8
HEAD_DIM = 128
ATTN_WIDTH = HEADS * HEAD_DIM
D_FF = 5632
PLE_DIM = 256
EPS = 1e-6
LANES = 128

VMEM_LIMIT = 56 * 1024 * 1024

F32 = jnp.float32
BF16 = jnp.bfloat16


def _params(*semantics):
    return pltpu.CompilerParams(dimension_semantics=semantics, vmem_limit_bytes=VMEM_LIMIT)


def _resident(shape):
    return pl.BlockSpec(shape, lambda *_: (0,) * len(shape), pipeline_mode=pl.Buffered(1))


def _dot(a, b):
    return jnp.dot(a, b, preferred_element_type=F32)


def _rms(xf, g):
    return xf * lax.rsqrt(jnp.mean(xf * xf, axis=-1, keepdims=True) + EPS) * g


def _gelu_erf(x):
    return 0.5 * x * (1.0 + lax.erf(x * (2.0 ** -0.5)))


def _gelu_tanh(x):
    return 0.5 * x * (1.0 + jnp.tanh((2.0 / jnp.pi) ** 0.5 * (x + 0.044715 * (x * x * x))))


def _sigmoid(x):
    return 1.0 / (1.0 + jnp.exp(-x))


def _log_sigmoid(x):
    return -(jnp.maximum(-x, 0.0) + jnp.log1p(jnp.exp(-jnp.abs(x))))


def _gmlp_kernel(x_ref, g_ref, w_ref, lng_ref, lnb_ref, ws_ref, bst_ref, oa_ref, h_ref):
    tm = x_ref.shape[0]
    h = _rms(x_ref[...], g_ref[...]).astype(BF16)
    h_ref[...] = h
    z = _dot(h, w_ref[...])
    u = _gelu_erf(z[:, :GMLP_WIDTH])
    v = _gelu_erf(z[:, GMLP_WIDTH:])
    mu = jnp.mean(v, axis=-1, keepdims=True)
    vc = v - mu
    vn = vc * lax.rsqrt(jnp.mean(vc * vc, axis=-1, keepdims=True) + EPS)
    vn = (vn * lng_ref[...] + lnb_ref[...]).astype(BF16)
    row = lax.broadcasted_iota(jnp.int32, (CHUNK, CHUNK), 0)
    col = lax.broadcasted_iota(jnp.int32, (CHUNK, CHUNK), 1)
    causal = row >= col
    for grp in range(GROUPS):
        cols = slice(grp * GROUP_DIM, (grp + 1) * GROUP_DIM)
        w = jnp.where(causal, ws_ref[grp], 0.0).astype(BF16)
        bias = bst_ref[:, grp:grp + 1]
        for c in range(tm // CHUNK):
            rows = slice(c * CHUNK, (c + 1) * CHUNK)
            mixed = _dot(w, vn[rows, cols]) + bias
            oa_ref[rows, cols] = (u[rows, cols] * mixed).astype(BF16)


def _gmlp(x, g, w_uv, ln_g, ln_b, w_s, b_s_t, tm):
    m = x.shape[0]
    row = lambda i: (i, 0)
    return pl.pallas_call(
        _gmlp_kernel,
        grid=(m // tm,),
        in_specs=[
            pl.BlockSpec((tm, D_MODEL), row),
            _resident((1, D_MODEL)),
            _resident((D_MODEL, 2 * GMLP_WIDTH)),
            _resident((1, GMLP_WIDTH)),
            _resident((1, GMLP_WIDTH)),
            _resident((GROUPS, CHUNK, CHUNK)),
            _resident((CHUNK, GROUPS)),
        ],
        out_specs=[pl.BlockSpec((tm, GMLP_WIDTH), row), pl.BlockSpec((tm, D_MODEL), row)],
        out_shape=[jax.ShapeDtypeStruct((m, GMLP_WIDTH), BF16), jax.ShapeDtypeStruct((m, D_MODEL), BF16)],
        compiler_params=_params("parallel"),
        name="gmlp",
    )(x, g, w_uv, ln_g, ln_b, w_s, b_s_t)


def _qkv_kernel(h_ref, w_ref, wf_ref, bf_ref, qg_ref, kg_ref, qkv_ref, cum_ref, cumt_ref, carry_ref,
                *, tiles_per_seq):
    tm = h_ref.shape[0]
    h = h_ref[...]
    for part, gain_ref in ((0, qg_ref), (1, kg_ref)):
        z = _dot(h, w_ref[:, part * ATTN_WIDTH:(part + 1) * ATTN_WIDTH])
        for hd in range(HEADS):
            cols = slice(hd * HEAD_DIM, (hd + 1) * HEAD_DIM)
            out_cols = slice(part * ATTN_WIDTH + hd * HEAD_DIM, part * ATTN_WIDTH + (hd + 1) * HEAD_DIM)
            qkv_ref[:, out_cols] = _rms(z[:, cols], gain_ref[...]).astype(BF16)
    qkv_ref[:, 2 * ATTN_WIDTH:] = _dot(h, w_ref[:, 2 * ATTN_WIDTH:]).astype(BF16)

    logf = _log_sigmoid(_dot(h, wf_ref[...]) + bf_ref[...])
    row = lax.broadcasted_iota(jnp.int32, (tm, tm), 0)
    col = lax.broadcasted_iota(jnp.int32, (tm, tm), 1)
    lower = jnp.where(row >= col, 1.0, 0.0).astype(BF16)
    hi = logf.astype(BF16)
    rest = logf - hi.astype(F32)
    mid = rest.astype(BF16)
    lo = (rest - mid.astype(F32)).astype(BF16)
    local = _dot(lower, hi) + _dot(lower, mid) + _dot(lower, lo)

    @pl.when(pl.program_id(0) % tiles_per_seq == 0)
    def _():
        carry_ref[...] = jnp.zeros_like(carry_ref)

    cum = local + carry_ref[...]
    carry_ref[...] = cum[tm - 1:tm, :]
    cum_ref[...] = cum
    cumt_ref[...] = cum.T[:HEADS, :]


def _qkv(h, w_qkv, w_f, b_f, q_g, k_g, tm, seq):
    m = h.shape[0]
    row = lambda i: (i, 0)
    return pl.pallas_call(
        functools.partial(_qkv_kernel, tiles_per_seq=seq // tm),
        grid=(m // tm,),
        in_specs=[
            pl.BlockSpec((tm, D_MODEL), row),
            _resident((D_MODEL, 3 * ATTN_WIDTH)),
            _resident((D_MODEL, LANES)),
            _resident((1, LANES)),
            _resident((1, HEAD_DIM)),
            _resident((1, HEAD_DIM)),
        ],
        out_specs=[
            pl.BlockSpec((tm, 3 * ATTN_WIDTH), row),
            pl.BlockSpec((tm, LANES), row),
            pl.BlockSpec((HEADS, tm), lambda i: (0, i)),
        ],
        out_shape=[
            jax.ShapeDtypeStruct((m, 3 * ATTN_WIDTH), BF16),
            jax.ShapeDtypeStruct((m, LANES), F32),
            jax.ShapeDtypeStruct((HEADS, m), F32),
        ],
        scratch_shapes=[pltpu.VMEM((1, LANES), F32)],
        compiler_params=_params("arbitrary"),
        name="qkv",
    )(h, w_qkv, w_f, b_f, q_g, k_g)


def _attn_kernel(q_ref, k_ref, v_ref, cum_ref, cumt_ref, o_ref, m_ref, l_ref, acc_ref):
    tq, tk = q_ref.shape[0], k_ref.shape[0]
    qi, ki = pl.program_id(1), pl.program_id(2)
    scale = HEAD_DIM ** -0.5

    @pl.when(ki == 0)
    def _():
        m_ref[...] = jnp.full_like(m_ref, -jnp.inf)
        l_ref[...] = jnp.zeros_like(l_ref)
        acc_ref[...] = jnp.zeros_like(acc_ref)

    def step(masked):
        if masked:
            row = lax.broadcasted_iota(jnp.int32, (tq, tk), 0)
            col = lax.broadcasted_iota(jnp.int32, (tq, tk), 1)
            keep = row >= col
        for hd in range(HEADS):
            cols = slice(hd * HEAD_DIM, (hd + 1) * HEAD_DIM)
            s = lax.dot_general(q_ref[:, cols], k_ref[:, cols], (((1,), (1,)), ((), ())),
                                preferred_element_type=F32)
            s = s * scale + cum_ref[:, hd:hd + 1] - cumt_ref[hd:hd + 1, :]
            if masked:
                s = jnp.where(keep, s, -jnp.inf)
            m_old = m_ref[hd]
            m_new = jnp.maximum(m_old, jnp.max(s, axis=-1, keepdims=True))
            alpha = jnp.exp(m_old - m_new)
            p = jnp.exp(s - m_new)
            l_ref[hd] = alpha * l_ref[hd] + jnp.sum(p, axis=-1, keepdims=True)
            acc_ref[:, cols] = alpha * acc_ref[:, cols] + _dot(p.astype(BF16), v_ref[:, cols])
            m_ref[hd] = m_new

    @pl.when(ki < qi)
    def _():
        step(masked=False)

    @pl.when(ki == qi)
    def _():
        step(masked=True)
        for hd in range(HEADS):
            cols = slice(hd * HEAD_DIM, (hd + 1) * HEAD_DIM)
            o_ref[:, cols] = (acc_ref[:, cols] / l_ref[hd]).astype(BF16)


def _attention(qkv, cum, cum_t, batch, seq, tq):
    m = qkv.shape[0]
    nq = seq // tq
    q_idx = lambda b, qi, ki: (b * nq + qi, 0)
    k_row = lambda b, qi, ki: b * nq + jnp.minimum(ki, qi)
    return pl.pallas_call(
        _attn_kernel,
        grid=(batch, nq, nq),
        in_specs=[
            pl.BlockSpec((tq, ATTN_WIDTH), q_idx),
            pl.BlockSpec((tq, ATTN_WIDTH), lambda b, qi, ki: (k_row(b, qi, ki), 1)),
            pl.BlockSpec((tq, ATTN_WIDTH), lambda b, qi, ki: (k_row(b, qi, ki), 2)),
            pl.BlockSpec((tq, LANES), q_idx),
            pl.BlockSpec((HEADS, tq), lambda b, qi, ki: (0, k_row(b, qi, ki))),
        ],
        out_specs=pl.BlockSpec((tq, ATTN_WIDTH), q_idx),
        out_shape=jax.ShapeDtypeStruct((m, ATTN_WIDTH), BF16),
        scratch_shapes=[
            pltpu.VMEM((HEADS, tq, 1), F32),
            pltpu.VMEM((HEADS, tq, 1), F32),
            pltpu.VMEM((tq, ATTN_WIDTH), F32),
        ],
        compiler_params=_params("parallel", "parallel", "arbitrary"),
        name="attention",
    )(qkv, qkv, qkv, cum, cum_t)


def _merge_kernel(h_ref, oa_ref, ob_ref, wga_ref, wgb_ref, wba_ref, wbb_ref, y_ref):
    h = h_ref[...]
    ya = _sigmoid(_dot(h, wga_ref[...])) * _dot(oa_ref[...], wba_ref[...])
    yb = _sigmoid(_dot(h, wgb_ref[...])) * _dot(ob_ref[...], wbb_ref[...])
    y_ref[...] = (ya + yb).astype(BF16)


def _merge(h, o_a, o_b, w_g, w_ba, w_bb, tm, tn):
    m = h.shape[0]
    nn = D_MODEL // tn
    row = lambda i, j: (i, 0)
    col = lambda i, j: (0, j)
    return pl.pallas_call(
        _merge_kernel,
        grid=(m // tm, nn),
        in_specs=[
            pl.BlockSpec((tm, D_MODEL), row),
            pl.BlockSpec((tm, GMLP_WIDTH), row),
            pl.BlockSpec((tm, ATTN_WIDTH), row),
            pl.BlockSpec((D_MODEL, tn), col),
            pl.BlockSpec((D_MODEL, tn), lambda i, j: (0, nn + j)),
            pl.BlockSpec((GMLP_WIDTH, tn), col),
            pl.BlockSpec((ATTN_WIDTH, tn), col),
        ],
        out_specs=pl.BlockSpec((tm, tn), lambda i, j: (i, j)),
        out_shape=jax.ShapeDtypeStruct((m, D_MODEL), BF16),
        compiler_params=_params("parallel", "arbitrary"),
        name="merge",
    )(h, o_a, o_b, w_g, w_g, w_ba, w_bb)


def _outproj_kernel(x_ref, y_ref, w_ref, g_ref, x1_ref, h2_ref):
    x1 = x_ref[...] + _dot(y_ref[...], w_ref[...])
    x1_ref[...] = x1
    h2_ref[...] = _rms(x1, g_ref[...]).astype(BF16)


def _outproj(x, y, w_out, g_next, tm):
    m = x.shape[0]
    row = lambda i: (i, 0)
    return pl.pallas_call(
        _outproj_kernel,
        grid=(m // tm,),
        in_specs=[
            pl.BlockSpec((tm, D_MODEL), row),
            pl.BlockSpec((tm, D_MODEL), row),
            _resident((D_MODEL, D_MODEL)),
            _resident((1, D_MODEL)),
        ],
        out_specs=[pl.BlockSpec((tm, D_MODEL), row), pl.BlockSpec((tm, D_MODEL), row)],
        out_shape=[jax.ShapeDtypeStruct((m, D_MODEL), F32), jax.ShapeDtypeStruct((m, D_MODEL), BF16)],
        compiler_params=_params("parallel"),
        name="outproj",
    )(x, y, w_out, g_next)


def _ffn_up_kernel(h_ref, wa_ref, wb_ref, cw_ref, cb_ref, o_ref):
    h = h_ref[...]
    a = _dot(h, wa_ref[...])
    t = lax.broadcasted_iota(jnp.int32, a.shape, 0)
    a1 = jnp.where(t >= 1, pltpu.roll(a, 1, 0), 0.0)
    a2 = jnp.where(t >= 2, pltpu.roll(a, 2, 0), 0.0)
    conv = cb_ref[...] + a2 * cw_ref[0:1, :] + a1 * cw_ref[1:2, :] + a * cw_ref[2:3, :]
    o_ref[...] = (_gelu_tanh(conv) * _dot(h, wb_ref[...])).astype(BF16)


def _ffn_up(h2, w_up, conv_w, conv_b, seq, tf):
    m = h2.shape[0]
    nf = D_FF // tf
    col = lambda b, j: (0, j)
    return pl.pallas_call(
        _ffn_up_kernel,
        grid=(m // seq, nf),
        in_specs=[
            pl.BlockSpec((seq, D_MODEL), lambda b, j: (b, 0)),
            pl.BlockSpec((D_MODEL, tf), col),
            pl.BlockSpec((D_MODEL, tf), lambda b, j: (0, nf + j)),
            pl.BlockSpec((conv_w.shape[0], tf), col),
            pl.BlockSpec((1, tf), col),
        ],
        out_specs=pl.BlockSpec((seq, tf), lambda b, j: (b, j)),
        out_shape=jax.ShapeDtypeStruct((m, D_FF), BF16),
        compiler_params=_params("parallel", "arbitrary"),
        name="ffn_up",
    )(h2, w_up, w_up, conv_w, conv_b)


def _ffn_down_kernel(x_ref, hid_ref, w_ref, g_ref, x2_ref, h3_ref):
    x2 = x_ref[...] + _dot(hid_ref[...], w_ref[...])
    x2_ref[...] = x2
    h3_ref[...] = _rms(x2, g_ref[...]).astype(BF16)


def _ffn_down(x1, hidden, w_down, g_next, tm):
    m = x1.shape[0]
    row = lambda i: (i, 0)
    return pl.pallas_call(
        _ffn_down_kernel,
        grid=(m // tm,),
        in_specs=[
            pl.BlockSpec((tm, D_MODEL), row),
            pl.BlockSpec((tm, D_FF), row),
            _resident((D_FF, D_MODEL)),
            _resident((1, D_MODEL)),
        ],
        out_specs=[pl.BlockSpec((tm, D_MODEL), row), pl.BlockSpec((tm, D_MODEL), row)],
        out_shape=[jax.ShapeDtypeStruct((m, D_MODEL), F32), jax.ShapeDtypeStruct((m, D_MODEL), BF16)],
        compiler_params=_params("parallel"),
        name="ffn_down",
    )(x1, hidden, w_down, g_next)


def _ple_kernel(x_ref, h_ref, p_ref, wg_ref, wp_ref, g_ref, o_ref):
    e = _rms(_dot(p_ref[...].astype(BF16), wp_ref[...]), g_ref[...])
    gate = _sigmoid(_dot(h_ref[...], wg_ref[...]))
    o_ref[...] = x_ref[...] + gate * e


def _ple(x2, h3, p, w_gate, w_proj, g, tm):
    m = x2.shape[0]
    row = lambda i: (i, 0)
    return pl.pallas_call(
        _ple_kernel,
        grid=(m // tm,),
        in_specs=[
            pl.BlockSpec((tm, D_MODEL), row),
            pl.BlockSpec((tm, D_MODEL), row),
            pl.BlockSpec((tm, PLE_DIM), row),
            _resident((D_MODEL, D_MODEL)),
            _resident((PLE_DIM, D_MODEL)),
            _resident((1, D_MODEL)),
        ],
        out_specs=pl.BlockSpec((tm, D_MODEL), row),
        out_shape=jax.ShapeDtypeStruct((m, D_MODEL), F32),
        compiler_params=_params("parallel"),
        name="ple",
    )(x2, h3, p, w_gate, w_proj, g)


def _layer(x, p, norm_mix_g, w_in, gmlp_ln_g, gmlp_ln_b, gmlp_w_s, gmlp_b_s, fox_b_f, q_norm_g,
           k_norm_g, w_branch_a, w_branch_b, w_out, norm_ffn_g, w_up, conv_w, conv_b, w_down,
           ple_proj, ple_norm_g, ple_gate_norm_g, w_ple_gate, *, batch, seq):
    row_vec = lambda v: v.reshape(1, -1)
    uv_end = 2 * GMLP_WIDTH
    qkv_end = uv_end + 3 * ATTN_WIDTH
    w_uv = w_in[:, :uv_end].astype(BF16)
    w_qkv = w_in[:, uv_end:qkv_end].astype(BF16)
    w_f = jnp.pad(w_in[:, qkv_end:qkv_end + HEADS], ((0, 0), (0, LANES - HEADS))).astype(BF16)
    b_f = jnp.pad(fox_b_f, (0, LANES - HEADS)).reshape(1, LANES)
    w_g = w_in[:, qkv_end + HEADS:].astype(BF16)

    o_a, h = _gmlp(x, row_vec(norm_mix_g), w_uv, row_vec(gmlp_ln_g), row_vec(gmlp_ln_b),
                   gmlp_w_s, gmlp_b_s.T, tm=512)
    qkv, cum, cum_t = _qkv(h, w_qkv, w_f, b_f, row_vec(q_norm_g), row_vec(k_norm_g), tm=512, seq=seq)
    o_b = _attention(qkv, cum, cum_t, batch, seq, tq=512)
    y = _merge(h, o_a, o_b, w_g, w_branch_a.astype(BF16), w_branch_b.astype(BF16), tm=1024, tn=512)
    x1, h2 = _outproj(x, y, w_out.astype(BF16), row_vec(norm_ffn_g), tm=512)
    hidden = _ffn_up(h2, w_up.astype(BF16), conv_w, row_vec(conv_b), seq=seq, tf=256)
    x2, h3 = _ffn_down(x1, hidden, w_down.astype(BF16), row_vec(ple_gate_norm_g), tm=256)
    return _ple(x2, h3, p, w_ple_gate.astype(BF16), ple_proj.astype(BF16), row_vec(ple_norm_g), tm=512)


def kernel(x, p, norm_mix_g, w_in, gmlp_ln_g, gmlp_ln_b, gmlp_w_s, gmlp_b_s, fox_b_f, q_norm_g, k_norm_g, w_branch_a, w_branch_b, w_out, norm_ffn_g, w_up, conv_w, conv_b, w_down, ple_proj, ple_norm_g, ple_gate_norm_g, w_ple_gate):
    batch, seq, d = x.shape
    depth = w_in.shape[0]
    xs = x.reshape(batch * seq, d)
    for i in range(depth):
        xs = _layer(xs, p[i].reshape(batch * seq, -1), norm_mix_g[i], w_in[i], gmlp_ln_g[i], gmlp_ln_b[i],
                    gmlp_w_s[i], gmlp_b_s[i], fox_b_f[i], q_norm_g[i], k_norm_g[i], w_branch_a[i],
                    w_branch_b[i], w_out[i], norm_ffn_g[i], w_up[i], conv_w[i], conv_b[i], w_down[i],
                    ple_proj[i], ple_norm_g[i], ple_gate_norm_g[i], w_ple_gate[i], batch=batch, seq=seq)
    return xs.reshape(batch, seq, d)
```

```python
import functools

import jax
import jax.numpy as jnp
import numpy as np
from jax import lax
from jax.experimental import pallas as pl
from jax.experimental.pallas import tpu as pltpu

D_MODEL = 2048
CHUNK = 128
GROUPS = 8
GROUP_DIM = 128
GMLP_WIDTH = GROUPS * GROUP_DIM
HEADS = 8
HEAD_DIM = 128
ATTN_WIDTH = HEADS * HEAD_DIM
D_FF = 5632
PLE_DIM = 256
EPS = 1e-6
LANES = 128
AUG = 2 * HEAD_DIM
N_SPLIT = 3
LOG2_E = 1.4426950408889634

VMEM_LIMIT = 56 * 1024 * 1024

F32 = jnp.float32
BF16 = jnp.bfloat16


def _params(*semantics):
    return pltpu.CompilerParams(dimension_semantics=semantics, vmem_limit_bytes=VMEM_LIMIT)


def _resident(shape):
    return pl.BlockSpec(shape, lambda *_: (0,) * len(shape), pipeline_mode=pl.Buffered(1))


def _dot(a, b):
    return jnp.dot(a, b, preferred_element_type=F32)


def _rms(xf, g):
    return xf * lax.rsqrt(jnp.mean(xf * xf, axis=-1, keepdims=True) + EPS) * g


def _gelu_erf(x):
    return 0.5 * x * (1.0 + lax.erf(x * (2.0 ** -0.5)))


def _gelu_tanh(x):
    return 0.5 * x * (1.0 + jnp.tanh((2.0 / jnp.pi) ** 0.5 * (x + 0.044715 * (x * x * x))))


def _sigmoid(x):
    return 1.0 / (1.0 + jnp.exp(-x))


def _log_sigmoid(x):
    return -(jnp.maximum(-x, 0.0) + jnp.log1p(jnp.exp(-jnp.abs(x))))


def _gmlp_kernel(x_ref, g_ref, w_ref, lng_ref, lnb_ref, ws_ref, bst_ref, oa_ref, h_ref):
    tm = x_ref.shape[0]
    h = _rms(x_ref[...], g_ref[...]).astype(BF16)
    h_ref[...] = h
    z = _dot(h, w_ref[...])
    u = _gelu_erf(z[:, :GMLP_WIDTH])
    v = _gelu_erf(z[:, GMLP_WIDTH:])
    mu = jnp.mean(v, axis=-1, keepdims=True)
    vc = v - mu
    vn = vc * lax.rsqrt(jnp.mean(vc * vc, axis=-1, keepdims=True) + EPS)
    vn = (vn * lng_ref[...] + lnb_ref[...]).astype(BF16)
    row = lax.broadcasted_iota(jnp.int32, (CHUNK, CHUNK), 0)
    col = lax.broadcasted_iota(jnp.int32, (CHUNK, CHUNK), 1)
    causal = row >= col
    for grp in range(GROUPS):
        cols = slice(grp * GROUP_DIM, (grp + 1) * GROUP_DIM)
        w = jnp.where(causal, ws_ref[grp], 0.0).astype(BF16)
        bias = bst_ref[:, grp:grp + 1]
        for c in range(tm // CHUNK):
            rows = slice(c * CHUNK, (c + 1) * CHUNK)
            mixed = _dot(w, vn[rows, cols]) + bias
            oa_ref[rows, cols] = (u[rows, cols] * mixed).astype(BF16)


def _gmlp(x, g, w_uv, ln_g, ln_b, w_s, b_s_t, tm):
    m = x.shape[0]
    row = lambda i: (i, 0)
    return pl.pallas_call(
        _gmlp_kernel,
        grid=(m // tm,),
        in_specs=[
            pl.BlockSpec((tm, D_MODEL), row),
            _resident((1, D_MODEL)),
            _resident((D_MODEL, 2 * GMLP_WIDTH)),
            _resident((1, GMLP_WIDTH)),
            _resident((1, GMLP_WIDTH)),
            _resident((GROUPS, CHUNK, CHUNK)),
            _resident((CHUNK, GROUPS)),
        ],
        out_specs=[pl.BlockSpec((tm, GMLP_WIDTH), row), pl.BlockSpec((tm, D_MODEL), row)],
        out_shape=[jax.ShapeDtypeStruct((m, GMLP_WIDTH), BF16), jax.ShapeDtypeStruct((m, D_MODEL), BF16)],
        compiler_params=_params("parallel"),
        name="gmlp",
    )(x, g, w_uv, ln_g, ln_b, w_s, b_s_t)


def _split_bf16(x):
    pieces = []
    for _ in range(N_SPLIT):
        piece = x.astype(BF16)
        pieces.append(piece)
        x = x - piece.astype(F32)
    return pieces


def _decay_constants(tm):
    lower = np.tril(np.ones((tm, tm), np.float32))
    sel_q = np.zeros((N_SPLIT * LANES, HEADS * HEAD_DIM), np.float32)
    sel_k = np.zeros_like(sel_q)
    ones_q = np.zeros((1, HEAD_DIM), np.float32)
    ones_k = np.zeros((1, HEAD_DIM), np.float32)
    for piece in range(N_SPLIT):
        ones_q[0, N_SPLIT + piece] = 1.0
        ones_k[0, piece] = 1.0
        for head in range(HEADS):
            sel_q[piece * LANES + head, head * HEAD_DIM + piece] = 1.0
            sel_k[piece * LANES + head, head * HEAD_DIM + N_SPLIT + piece] = -1.0
    as_bf16 = lambda a: jnp.asarray(a, BF16)
    return as_bf16(lower), as_bf16(sel_q), as_bf16(sel_k), jnp.asarray(ones_q), jnp.asarray(ones_k)


def _qkv_kernel(h_ref, w_ref, wf_ref, bf_ref, qg_ref, kg_ref, lower_ref, selq_ref, selk_ref, oneq_ref, onek_ref,
                qa_ref, ka_ref, v_ref, carry_ref, *, tiles_per_seq):
    tm = h_ref.shape[0]
    h = h_ref[...]

    logf = _log_sigmoid(_dot(h, wf_ref[...]) + bf_ref[...])
    lower = lower_ref[...]
    local = sum(_dot(lower, piece) for piece in _split_bf16(logf))

    @pl.when(pl.program_id(0) % tiles_per_seq == 0)
    def _():
        carry_ref[...] = jnp.zeros_like(carry_ref)

    cum = local + carry_ref[...]
    carry_ref[...] = cum[tm - 1:tm, :]
    pieces = jnp.concatenate(_split_bf16(cum * (HEAD_DIM ** 0.5)), axis=1)

    for z, gain_ref, sel_ref, ones_ref, out_ref in (
            (_dot(h, w_ref[:, :ATTN_WIDTH]), qg_ref, selq_ref, oneq_ref, qa_ref),
            (_dot(h, w_ref[:, ATTN_WIDTH:2 * ATTN_WIDTH]), kg_ref, selk_ref, onek_ref, ka_ref)):
        decay = _dot(pieces, sel_ref[...])
        for hd in range(HEADS):
            cols = slice(hd * HEAD_DIM, (hd + 1) * HEAD_DIM)
            out_ref[:, hd * AUG:hd * AUG + HEAD_DIM] = _rms(z[:, cols], gain_ref[...]).astype(BF16)
            out_ref[:, hd * AUG + HEAD_DIM:(hd + 1) * AUG] = (decay[:, cols] + ones_ref[...]).astype(BF16)
    v_ref[...] = _dot(h, w_ref[:, 2 * ATTN_WIDTH:]).astype(BF16)


def _qkv(h, w_qkv, w_f, b_f, q_g, k_g, tm, seq):
    m = h.shape[0]
    row = lambda i: (i, 0)
    consts = _decay_constants(tm)
    return pl.pallas_call(
        functools.partial(_qkv_kernel, tiles_per_seq=seq // tm),
        grid=(m // tm,),
        in_specs=[
            pl.BlockSpec((tm, D_MODEL), row),
            _resident((D_MODEL, 3 * ATTN_WIDTH)),
            _resident((D_MODEL, LANES)),
            _resident((1, LANES)),
            _resident((1, HEAD_DIM)),
            _resident((1, HEAD_DIM)),
        ] + [_resident(c.shape) for c in consts],
        out_specs=[
            pl.BlockSpec((tm, HEADS * AUG), row),
            pl.BlockSpec((tm, HEADS * AUG), row),
            pl.BlockSpec((tm, ATTN_WIDTH), row),
        ],
        out_shape=[
            jax.ShapeDtypeStruct((m, HEADS * AUG), BF16),
            jax.ShapeDtypeStruct((m, HEADS * AUG), BF16),
            jax.ShapeDtypeStruct((m, ATTN_WIDTH), BF16),
        ],
        scratch_shapes=[pltpu.VMEM((1, LANES), F32)],
        compiler_params=_params("arbitrary"),
        name="qkv",
    )(h, w_qkv, w_f, b_f, q_g, k_g, *consts)


def _attn_kernel(q_ref, k_ref, v_ref, o_ref, *, tq):
    seq = q_ref.shape[0]
    to_log2 = (HEAD_DIM ** -0.5) * LOG2_E
    row = lax.broadcasted_iota(jnp.int32, (tq, tq), 0)
    col = lax.broadcasted_iota(jnp.int32, (tq, tq), 1)
    keep = row >= col
    v_aug = jnp.concatenate([v_ref[...], jnp.ones((seq, HEAD_DIM), BF16)], axis=1)
    contract_last = (((1,), (1,)), ((), ()))
    for qi in range(seq // tq):
        q0, q1 = qi * tq, (qi + 1) * tq
        q = q_ref[q0:q1, :]
        diag = lax.dot_general(q, k_ref[q0:q1, :], contract_last, preferred_element_type=F32)
        diag = jnp.where(keep, diag, -jnp.inf)
        m = jnp.max(diag, axis=-1, keepdims=True)
        if qi:
            past = lax.dot_general(q, k_ref[:q0, :], contract_last, preferred_element_type=F32)
            m = jnp.maximum(m, jnp.max(past, axis=-1, keepdims=True))
        acc = _dot(jnp.exp2((diag - m) * to_log2).astype(BF16), v_aug[q0:q1, :])
        if qi:
            acc = acc + _dot(jnp.exp2((past - m) * to_log2).astype(BF16), v_aug[:q0, :])
        o_ref[q0:q1, :] = (acc[:, :HEAD_DIM] / acc[:, HEAD_DIM:]).astype(BF16)


def _attention(q_aug, k_aug, v, batch, seq, tq):
    m = v.shape[0]
    head_block = lambda b, hd: (b, hd)
    return pl.pallas_call(
        functools.partial(_attn_kernel, tq=tq),
        grid=(batch, HEADS),
        in_specs=[
            pl.BlockSpec((seq, AUG), head_block),
            pl.BlockSpec((seq, AUG), head_block),
            pl.BlockSpec((seq, HEAD_DIM), head_block),
        ],
        out_specs=pl.BlockSpec((seq, HEAD_DIM), head_block),
        out_shape=jax.ShapeDtypeStruct((m, ATTN_WIDTH), BF16),
        compiler_params=_params("parallel", "parallel"),
        name="attention",
    )(q_aug, k_aug, v)


def _merge_kernel(h_ref, oa_ref, ob_ref, wga_ref, wgb_ref, wba_ref, wbb_ref, y_ref):
    h = h_ref[...]
    ya = _sigmoid(_dot(h, wga_ref[...])) * _dot(oa_ref[...], wba_ref[...])
    yb = _sigmoid(_dot(h, wgb_ref[...])) * _dot(ob_ref[...], wbb_ref[...])
    y_ref[...] = (ya + yb).astype(BF16)


def _merge(h, o_a, o_b, w_g, w_ba, w_bb, tm, tn):
    m = h.shape[0]
    nn = D_MODEL // tn
    row = lambda i, j: (i, 0)
    col = lambda i, j: (0, j)
    return pl.pallas_call(
        _merge_kernel,
        grid=(m // tm, nn),
        in_specs=[
            pl.BlockSpec((tm, D_MODEL), row),
            pl.BlockSpec((tm, GMLP_WIDTH), row),
            pl.BlockSpec((tm, ATTN_WIDTH), row),
            pl.BlockSpec((D_MODEL, tn), col),
            pl.BlockSpec((D_MODEL, tn), lambda i, j: (0, nn + j)),
            pl.BlockSpec((GMLP_WIDTH, tn), col),
            pl.BlockSpec((ATTN_WIDTH, tn), col),
        ],
        out_specs=pl.BlockSpec((tm, tn), lambda i, j: (i, j)),
        out_shape=jax.ShapeDtypeStruct((m, D_MODEL), BF16),
        compiler_params=_params("parallel", "arbitrary"),
        name="merge",
    )(h, o_a, o_b, w_g, w_g, w_ba, w_bb)


def _outproj_kernel(x_ref, y_ref, w_ref, g_ref, x1_ref, h2_ref):
    x1 = x_ref[...] + _dot(y_ref[...], w_ref[...])
    x1_ref[...] = x1
    h2_ref[...] = _rms(x1, g_ref[...]).astype(BF16)


def _outproj(x, y, w_out, g_next, tm):
    m = x.shape[0]
    row = lambda i: (i, 0)
    return pl.pallas_call(
        _outproj_kernel,
        grid=(m // tm,),
        in_specs=[
            pl.BlockSpec((tm, D_MODEL), row),
            pl.BlockSpec((tm, D_MODEL), row),
            _resident((D_MODEL, D_MODEL)),
            _resident((1, D_MODEL)),
        ],
        out_specs=[pl.BlockSpec((tm, D_MODEL), row), pl.BlockSpec((tm, D_MODEL), row)],
        out_shape=[jax.ShapeDtypeStruct((m, D_MODEL), F32), jax.ShapeDtypeStruct((m, D_MODEL), BF16)],
        compiler_params=_params("parallel"),
        name="outproj",
    )(x, y, w_out, g_next)


def _ffn_up_kernel(h_ref, wa_ref, wb_ref, cw_ref, cb_ref, o_ref):
    h = h_ref[...]
    a = _dot(h, wa_ref[...])
    t = lax.broadcasted_iota(jnp.int32, a.shape, 0)
    a1 = jnp.where(t >= 1, pltpu.roll(a, 1, 0), 0.0)
    a2 = jnp.where(t >= 2, pltpu.roll(a, 2, 0), 0.0)
    conv = cb_ref[...] + a2 * cw_ref[0:1, :] + a1 * cw_ref[1:2, :] + a * cw_ref[2:3, :]
    o_ref[...] = (_gelu_tanh(conv) * _dot(h, wb_ref[...])).astype(BF16)


def _ffn_up(h2, w_up, conv_w, conv_b, seq, tf):
    m = h2.shape[0]
    nf = D_FF // tf
    col = lambda b, j: (0, j)
    return pl.pallas_call(
        _ffn_up_kernel,
        grid=(m // seq, nf),
        in_specs=[
            pl.BlockSpec((seq, D_MODEL), lambda b, j: (b, 0)),
            pl.BlockSpec((D_MODEL, tf), col),
            pl.BlockSpec((D_MODEL, tf), lambda b, j: (0, nf + j)),
            pl.BlockSpec((conv_w.shape[0], tf), col),
            pl.BlockSpec((1, tf), col),
        ],
        out_specs=pl.BlockSpec((seq, tf), lambda b, j: (b, j)),
        out_shape=jax.ShapeDtypeStruct((m, D_FF), BF16),
        compiler_params=_params("parallel", "arbitrary"),
        name="ffn_up",
    )(h2, w_up, w_up, conv_w, conv_b)


def _ffn_down_kernel(x_ref, hid_ref, w_ref, g_ref, x2_ref, h3_ref):
    x2 = x_ref[...] + _dot(hid_ref[...], w_ref[...])
    x2_ref[...] = x2
    h3_ref[...] = _rms(x2, g_ref[...]).astype(BF16)


def _ffn_down(x1, hidden, w_down, g_next, tm):
    m = x1.shape[0]
    row = lambda i: (i, 0)
    return pl.pallas_call(
        _ffn_down_kernel,
        grid=(m // tm,),
        in_specs=[
            pl.BlockSpec((tm, D_MODEL), row),
            pl.BlockSpec((tm, D_FF), row),
            _resident((D_FF, D_MODEL)),
            _resident((1, D_MODEL)),
        ],
        out_specs=[pl.BlockSpec((tm, D_MODEL), row), pl.BlockSpec((tm, D_MODEL), row)],
        out_shape=[jax.ShapeDtypeStruct((m, D_MODEL), F32), jax.ShapeDtypeStruct((m, D_MODEL), BF16)],
        compiler_params=_params("parallel"),
        name="ffn_down",
    )(x1, hidden, w_down, g_next)


def _ple_kernel(x_ref, h_ref, p_ref, wg_ref, wp_ref, g_ref, o_ref):
    e = _rms(_dot(p_ref[...].astype(BF16), wp_ref[...]), g_ref[...])
    gate = _sigmoid(_dot(h_ref[...], wg_ref[...]))
    o_ref[...] = x_ref[...] + gate * e


def _ple(x2, h3, p, w_gate, w_proj, g, tm):
    m = x2.shape[0]
    row = lambda i: (i, 0)
    return pl.pallas_call(
        _ple_kernel,
        grid=(m // tm,),
        in_specs=[
            pl.BlockSpec((tm, D_MODEL), row),
            pl.BlockSpec((tm, D_MODEL), row),
            pl.BlockSpec((tm, PLE_DIM), row),
            _resident((D_MODEL, D_MODEL)),
            _resident((PLE_DIM, D_MODEL)),
            _resident((1, D_MODEL)),
        ],
        out_specs=pl.BlockSpec((tm, D_MODEL), row),
        out_shape=jax.ShapeDtypeStruct((m, D_MODEL), F32),
        compiler_params=_params("parallel"),
        name="ple",
    )(x2, h3, p, w_gate, w_proj, g)


def _layer(x, p, norm_mix_g, w_in, gmlp_ln_g, gmlp_ln_b, gmlp_w_s, gmlp_b_s, fox_b_f, q_norm_g,
           k_norm_g, w_branch_a, w_branch_b, w_out, norm_ffn_g, w_up, conv_w, conv_b, w_down,
           ple_proj, ple_norm_g, ple_gate_norm_g, w_ple_gate, *, batch, seq):
    row_vec = lambda v: v.reshape(1, -1)
    uv_end = 2 * GMLP_WIDTH
    qkv_end = uv_end + 3 * ATTN_WIDTH
    w_uv = w_in[:, :uv_end].astype(BF16)
    w_qkv = w_in[:, uv_end:qkv_end].astype(BF16)
    w_f = jnp.pad(w_in[:, qkv_end:qkv_end + HEADS], ((0, 0), (0, LANES - HEADS))).astype(BF16)
    b_f = jnp.pad(fox_b_f, (0, LANES - HEADS)).reshape(1, LANES)
    w_g = w_in[:, qkv_end + HEADS:].astype(BF16)

    o_a, h = _gmlp(x, row_vec(norm_mix_g), w_uv, row_vec(gmlp_ln_g), row_vec(gmlp_ln_b),
                   gmlp_w_s, gmlp_b_s.T, tm=512)
    q_aug, k_aug, v = _qkv(h, w_qkv, w_f, b_f, row_vec(q_norm_g), row_vec(k_norm_g), tm=512, seq=seq)
    o_b = _attention(q_aug, k_aug, v, batch, seq, tq=512)
    y = _merge(h, o_a, o_b, w_g, w_branch_a.astype(BF16), w_branch_b.astype(BF16), tm=1024, tn=512)
    x1, h2 = _outproj(x, y, w_out.astype(BF16), row_vec(norm_ffn_g), tm=512)
    hidden = _ffn_up(h2, w_up.astype(BF16), conv_w, row_vec(conv_b), seq=seq, tf=256)
    x2, h3 = _ffn_down(x1, hidden, w_down.astype(BF16), row_vec(ple_gate_norm_g), tm=256)
    return _ple(x2, h3, p, w_ple_gate.astype(BF16), ple_proj.astype(BF16), row_vec(ple_norm_g), tm=512)


def kernel(x, p, norm_mix_g, w_in, gmlp_ln_g, gmlp_ln_b, gmlp_w_s, gmlp_b_s, fox_b_f, q_norm_g, k_norm_g, w_branch_a, w_branch_b, w_out, norm_ffn_g, w_up, conv_w, conv_b, w_down, ple_proj, ple_norm_g, ple_gate_norm_g, w_ple_gate):
    batch, seq, d = x.shape
    depth = w_in.shape[0]
    xs = x.reshape(batch * seq, d)
    for i in range(depth):
        xs = _layer(xs, p[i].reshape(batch * seq, -1), norm_mix_g[i], w_in[i], gmlp_ln_g[i], gmlp_ln_b[i],
                    gmlp_w_s[i], gmlp_b_s[i], fox_b_f[i], q_norm_g[i], k_norm_g[i], w_branch_a[i],
                    w_branch_b[i], w_out[i], norm_ffn_g[i], w_up[i], conv_w[i], conv_b[i], w_down[i],
                    ple_proj[i], ple_norm_g[i], ple_gate_norm_g[i], w_ple_gate[i], batch=batch, seq=seq)
    return xs.reshape(batch, seq, d)
```

```python
import functools

import jax
import jax.numpy as jnp
import numpy as np
from jax import lax
from jax.experimental import pallas as pl
from jax.experimental.pallas import tpu as pltpu

D_MODEL = 2048
CHUNK = 128
GROUPS = 8
GROUP_DIM = 128
GMLP_WIDTH = GROUPS * GROUP_DIM
HEADS = 8
HEAD_DIM = 128
ATTN_WIDTH = HEADS * HEAD_DIM
D_FF = 5632
PLE_DIM = 256
EPS = 1e-6
Q_COLS = 2 * GMLP_WIDTH
F_COLS = Q_COLS + 3 * ATTN_WIDTH
G_COLS = F_COLS + HEADS
LANES = 128
AUG = 2 * HEAD_DIM
N_SPLIT = 3
LOG2_E = 1.4426950408889634
DOT_ROWS = 512

VMEM_LIMIT = 56 * 1024 * 1024

F32 = jnp.float32
BF16 = jnp.bfloat16


def _params(*semantics):
    return pltpu.CompilerParams(dimension_semantics=semantics, vmem_limit_bytes=VMEM_LIMIT)


def _resident(shape):
    return pl.BlockSpec(shape, lambda *_: (0,) * len(shape), pipeline_mode=pl.Buffered(1))


def _resident_cols(block_shape, col_block):
    return pl.BlockSpec(block_shape, lambda *_: (0, col_block), pipeline_mode=pl.Buffered(1))


def _dot(a, b):
    return jnp.dot(a, b, preferred_element_type=F32)


def _rms(xf, g):
    return xf * lax.rsqrt(jnp.mean(xf * xf, axis=-1, keepdims=True) + EPS) * g


def _gelu_erf(x):
    return 0.5 * x * (1.0 + lax.erf(x * (2.0 ** -0.5)))


def _gelu_tanh(x):
    return 0.5 * x * (1.0 + jnp.tanh((2.0 / jnp.pi) ** 0.5 * (x + 0.044715 * (x * x * x))))


def _sigmoid(x):
    return 1.0 / (1.0 + jnp.exp(-x))


def _log_sigmoid(x):
    return -(jnp.maximum(-x, 0.0) + jnp.log1p(jnp.exp(-jnp.abs(x))))


def _gmlp_kernel(x_ref, g_ref, w_ref, lng_ref, lnb_ref, ws_ref, bst_ref, oa_ref, h_ref):
    tm = x_ref.shape[0]
    h = _rms(x_ref[...], g_ref[...]).astype(BF16)
    h_ref[...] = h
    z = _dot(h, w_ref[...])
    u = _gelu_erf(z[:, :GMLP_WIDTH])
    v = _gelu_erf(z[:, GMLP_WIDTH:])
    mu = jnp.mean(v, axis=-1, keepdims=True)
    vc = v - mu
    vn = vc * lax.rsqrt(jnp.mean(vc * vc, axis=-1, keepdims=True) + EPS)
    vn = (vn * lng_ref[...] + lnb_ref[...]).astype(BF16)
    row = lax.broadcasted_iota(jnp.int32, (CHUNK, CHUNK), 0)
    col = lax.broadcasted_iota(jnp.int32, (CHUNK, CHUNK), 1)
    causal = row >= col
    for grp in range(GROUPS):
        cols = slice(grp * GROUP_DIM, (grp + 1) * GROUP_DIM)
        w = jnp.where(causal, ws_ref[grp], 0.0).astype(BF16)
        bias = bst_ref[:, grp:grp + 1]
        for c in range(tm // CHUNK):
            rows = slice(c * CHUNK, (c + 1) * CHUNK)
            mixed = _dot(w, vn[rows, cols]) + bias
            oa_ref[rows, cols] = (u[rows, cols] * mixed).astype(BF16)


def _gmlp(x, g, w_uv, ln_g, ln_b, w_s, b_s_t, tm):
    m = x.shape[0]
    row = lambda i: (i, 0)
    return pl.pallas_call(
        _gmlp_kernel,
        grid=(m // tm,),
        in_specs=[
            pl.BlockSpec((tm, D_MODEL), row),
            _resident((1, D_MODEL)),
            _resident_cols((D_MODEL, 2 * GMLP_WIDTH), 0),
            _resident((1, GMLP_WIDTH)),
            _resident((1, GMLP_WIDTH)),
            _resident((GROUPS, CHUNK, CHUNK)),
            _resident((CHUNK, GROUPS)),
        ],
        out_specs=[pl.BlockSpec((tm, GMLP_WIDTH), row), pl.BlockSpec((tm, D_MODEL), row)],
        out_shape=[jax.ShapeDtypeStruct((m, GMLP_WIDTH), BF16), jax.ShapeDtypeStruct((m, D_MODEL), BF16)],
        compiler_params=_params("parallel"),
        name="gmlp",
    )(x, g, w_uv, ln_g, ln_b, w_s, b_s_t)


def _split_bf16(x):
    pieces = []
    for _ in range(N_SPLIT):
        piece = x.astype(BF16)
        pieces.append(piece)
        x = x - piece.astype(F32)
    return pieces


def _decay_constants(tm):
    lower = np.tril(np.ones((tm, tm), np.float32))
    sel_q = np.zeros((N_SPLIT * LANES, HEADS * HEAD_DIM), np.float32)
    sel_k = np.zeros_like(sel_q)
    ones_q = np.zeros((1, HEAD_DIM), np.float32)
    ones_k = np.zeros((1, HEAD_DIM), np.float32)
    for piece in range(N_SPLIT):
        ones_q[0, N_SPLIT + piece] = 1.0
        ones_k[0, piece] = 1.0
        for head in range(HEADS):
            sel_q[piece * LANES + head, head * HEAD_DIM + piece] = 1.0
            sel_k[piece * LANES + head, head * HEAD_DIM + N_SPLIT + piece] = -1.0
    as_bf16 = lambda a: jnp.asarray(a, BF16)
    return as_bf16(lower), as_bf16(sel_q), as_bf16(sel_k), jnp.asarray(ones_q), jnp.asarray(ones_k)


def _qkv_kernel(h_ref, wq_ref, wk_ref, wv_ref, wf_ref, bf_ref, qg_ref, kg_ref, lower_ref, selq_ref, selk_ref, oneq_ref, onek_ref,
                qa_ref, ka_ref, v_ref, carry_ref, *, tiles_per_seq):
    tm = h_ref.shape[0]
    h = h_ref[...]

    logf = _log_sigmoid(_dot(h, wf_ref[...]) + bf_ref[...])
    lower = lower_ref[...]
    local = sum(_dot(lower, piece) for piece in _split_bf16(logf))

    @pl.when(pl.program_id(0) % tiles_per_seq == 0)
    def _():
        carry_ref[...] = jnp.zeros_like(carry_ref)

    cum = local + carry_ref[...]
    carry_ref[...] = cum[tm - 1:tm, :]
    pieces = jnp.concatenate(_split_bf16(cum * (HEAD_DIM ** 0.5)), axis=1)

    for z, gain_ref, sel_ref, ones_ref, out_ref in (
            (_dot(h, wq_ref[...]), qg_ref, selq_ref, oneq_ref, qa_ref),
            (_dot(h, wk_ref[...]), kg_ref, selk_ref, onek_ref, ka_ref)):
        decay = _dot(pieces, sel_ref[...])
        for hd in range(HEADS):
            cols = slice(hd * HEAD_DIM, (hd + 1) * HEAD_DIM)
            out_ref[:, hd * AUG:hd * AUG + HEAD_DIM] = _rms(z[:, cols], gain_ref[...]).astype(BF16)
            out_ref[:, hd * AUG + HEAD_DIM:(hd + 1) * AUG] = (decay[:, cols] + ones_ref[...]).astype(BF16)
    v_ref[...] = _dot(h, wv_ref[...]).astype(BF16)


def _qkv(h, w_in, b_f, q_g, k_g, tm, seq):
    m = h.shape[0]
    row = lambda i: (i, 0)
    consts = _decay_constants(tm)
    return pl.pallas_call(
        functools.partial(_qkv_kernel, tiles_per_seq=seq // tm),
        grid=(m // tm,),
        in_specs=[
            pl.BlockSpec((tm, D_MODEL), row),
            _resident_cols((D_MODEL, ATTN_WIDTH), Q_COLS // ATTN_WIDTH),
            _resident_cols((D_MODEL, ATTN_WIDTH), Q_COLS // ATTN_WIDTH + 1),
            _resident_cols((D_MODEL, ATTN_WIDTH), Q_COLS // ATTN_WIDTH + 2),
            _resident_cols((D_MODEL, LANES), F_COLS // LANES),
            _resident((1, LANES)),
            _resident((1, HEAD_DIM)),
            _resident((1, HEAD_DIM)),
        ] + [_resident(c.shape) for c in consts],
        out_specs=[
            pl.BlockSpec((tm, HEADS * AUG), row),
            pl.BlockSpec((tm, HEADS * AUG), row),
            pl.BlockSpec((tm, ATTN_WIDTH), row),
        ],
        out_shape=[
            jax.ShapeDtypeStruct((m, HEADS * AUG), BF16),
            jax.ShapeDtypeStruct((m, HEADS * AUG), BF16),
            jax.ShapeDtypeStruct((m, ATTN_WIDTH), BF16),
        ],
        scratch_shapes=[pltpu.VMEM((1, LANES), F32)],
        compiler_params=_params("arbitrary"),
        name="qkv",
    )(h, w_in, w_in, w_in, w_in, b_f, q_g, k_g, *consts)


def _attn_kernel(q_ref, k_ref, v_ref, o_ref, *, tq):
    seq = q_ref.shape[0]
    to_log2 = (HEAD_DIM ** -0.5) * LOG2_E
    row = lax.broadcasted_iota(jnp.int32, (tq, tq), 0)
    col = lax.broadcasted_iota(jnp.int32, (tq, tq), 1)
    keep = row >= col
    v_aug = jnp.concatenate([v_ref[...], jnp.ones((seq, HEAD_DIM), BF16)], axis=1)
    contract_last = (((1,), (1,)), ((), ()))
    for qi in range(seq // tq):
        q0, q1 = qi * tq, (qi + 1) * tq
        q = q_ref[q0:q1, :]
        diag = lax.dot_general(q, k_ref[q0:q1, :], contract_last, preferred_element_type=F32)
        diag = jnp.where(keep, diag, -jnp.inf)
        m = jnp.max(diag, axis=-1, keepdims=True)
        if qi:
            past = lax.dot_general(q, k_ref[:q0, :], contract_last, preferred_element_type=F32)
            m = jnp.maximum(m, jnp.max(past, axis=-1, keepdims=True))
        acc = _dot(jnp.exp2((diag - m) * to_log2).astype(BF16), v_aug[q0:q1, :])
        if qi:
            acc = acc + _dot(jnp.exp2((past - m) * to_log2).astype(BF16), v_aug[:q0, :])
        o_ref[q0:q1, :] = (acc[:, :HEAD_DIM] / acc[:, HEAD_DIM:]).astype(BF16)


def _attention(q_aug, k_aug, v, batch, seq, tq):
    m = v.shape[0]
    head_block = lambda b, hd: (b, hd)
    return pl.pallas_call(
        functools.partial(_attn_kernel, tq=tq),
        grid=(batch, HEADS),
        in_specs=[
            pl.BlockSpec((seq, AUG), head_block),
            pl.BlockSpec((seq, AUG), head_block),
            pl.BlockSpec((seq, HEAD_DIM), head_block),
        ],
        out_specs=pl.BlockSpec((seq, HEAD_DIM), head_block),
        out_shape=jax.ShapeDtypeStruct((m, ATTN_WIDTH), BF16),
        compiler_params=_params("parallel", "parallel"),
        name="attention",
    )(q_aug, k_aug, v)


def _merge_kernel(h_ref, oa_ref, ob_ref, wga_ref, wgb_ref, wba_ref, wbb_ref, y_ref):
    h = h_ref[...]
    ya = _sigmoid(_dot(h, wga_ref[...])) * _dot(oa_ref[...], wba_ref[...])
    yb = _sigmoid(_dot(h, wgb_ref[...])) * _dot(ob_ref[...], wbb_ref[...])
    y_ref[...] = (ya + yb).astype(BF16)


def _merge(h, o_a, o_b, w_g, w_ba, w_bb, tm, tn):
    m = h.shape[0]
    nn = D_MODEL // tn
    row = lambda i, j: (i, 0)
    col = lambda i, j: (0, j)
    return pl.pallas_call(
        _merge_kernel,
        grid=(m // tm, nn),
        in_specs=[
            pl.BlockSpec((tm, D_MODEL), row),
            pl.BlockSpec((tm, GMLP_WIDTH), row),
            pl.BlockSpec((tm, ATTN_WIDTH), row),
            pl.BlockSpec((D_MODEL, tn), col),
            pl.BlockSpec((D_MODEL, tn), lambda i, j: (0, nn + j)),
            pl.BlockSpec((GMLP_WIDTH, tn), col),
            pl.BlockSpec((ATTN_WIDTH, tn), col),
        ],
        out_specs=pl.BlockSpec((tm, tn), lambda i, j: (i, j)),
        out_shape=jax.ShapeDtypeStruct((m, D_MODEL), BF16),
        compiler_params=_params("parallel", "arbitrary"),
        name="merge",
    )(h, o_a, o_b, w_g, w_g, w_ba, w_bb)


def _outproj_kernel(x_ref, y_ref, w_ref, g_ref, x1_ref, h2_ref):
    x1 = x_ref[...] + _dot(y_ref[...], w_ref[...])
    x1_ref[...] = x1
    h2_ref[...] = _rms(x1, g_ref[...]).astype(BF16)


def _outproj(x, y, w_out, g_next, tm):
    m = x.shape[0]
    row = lambda i: (i, 0)
    return pl.pallas_call(
        _outproj_kernel,
        grid=(m // tm,),
        in_specs=[
            pl.BlockSpec((tm, D_MODEL), row),
            pl.BlockSpec((tm, D_MODEL), row),
            _resident((D_MODEL, D_MODEL)),
            _resident((1, D_MODEL)),
        ],
        out_specs=[pl.BlockSpec((tm, D_MODEL), row), pl.BlockSpec((tm, D_MODEL), row)],
        out_shape=[jax.ShapeDtypeStruct((m, D_MODEL), F32), jax.ShapeDtypeStruct((m, D_MODEL), BF16)],
        compiler_params=_params("parallel"),
        name="outproj",
    )(x, y, w_out, g_next)


def _ffn_up_kernel(h_ref, wa_ref, wb_ref, cw_ref, cb_ref, o_ref):
    seq = h_ref.shape[0]

    def proj(w):
        return jnp.concatenate([_dot(h_ref[r:r + DOT_ROWS, :], w) for r in range(0, seq, DOT_ROWS)], axis=0)

    a = proj(wa_ref[...])
    t = lax.broadcasted_iota(jnp.int32, a.shape, 0)
    a1 = jnp.where(t >= 1, pltpu.roll(a, 1, 0), 0.0)
    a2 = jnp.where(t >= 2, pltpu.roll(a, 2, 0), 0.0)
    conv = cb_ref[...] + a2 * cw_ref[0:1, :] + a1 * cw_ref[1:2, :] + a * cw_ref[2:3, :]
    o_ref[...] = (_gelu_tanh(conv) * proj(wb_ref[...])).astype(BF16)


def _ffn_up(h2, w_up, conv_w, conv_b, seq, tf):
    m = h2.shape[0]
    nf = D_FF // tf
    col = lambda b, j: (0, j)
    return pl.pallas_call(
        _ffn_up_kernel,
        grid=(m // seq, nf),
        in_specs=[
            pl.BlockSpec((seq, D_MODEL), lambda b, j: (b, 0)),
            pl.BlockSpec((D_MODEL, tf), col),
            pl.BlockSpec((D_MODEL, tf), lambda b, j: (0, nf + j)),
            pl.BlockSpec((conv_w.shape[0], tf), col),
            pl.BlockSpec((1, tf), col),
        ],
        out_specs=pl.BlockSpec((seq, tf), lambda b, j: (b, j)),
        out_shape=jax.ShapeDtypeStruct((m, D_FF), BF16),
        compiler_params=_params("parallel", "arbitrary"),
        name="ffn_up",
    )(h2, w_up, w_up, conv_w, conv_b)


def _ffn_down_kernel(x_ref, hid_ref, w_ref, g_ref, x2_ref, h3_ref):
    x2 = x_ref[...] + _dot(hid_ref[...], w_ref[...])
    x2_ref[...] = x2
    h3_ref[...] = _rms(x2, g_ref[...]).astype(BF16)


def _ffn_down(x1, hidden, w_down, g_next, tm):
    m = x1.shape[0]
    row = lambda i: (i, 0)
    return pl.pallas_call(
        _ffn_down_kernel,
        grid=(m // tm,),
        in_specs=[
            pl.BlockSpec((tm, D_MODEL), row),
            pl.BlockSpec((tm, D_FF), row),
            _resident((D_FF, D_MODEL)),
            _resident((1, D_MODEL)),
        ],
        out_specs=[pl.BlockSpec((tm, D_MODEL), row), pl.BlockSpec((tm, D_MODEL), row)],
        out_shape=[jax.ShapeDtypeStruct((m, D_MODEL), F32), jax.ShapeDtypeStruct((m, D_MODEL), BF16)],
        compiler_params=_params("parallel"),
        name="ffn_down",
    )(x1, hidden, w_down, g_next)


def _ple_kernel(x_ref, h_ref, p_ref, wg_ref, wp_ref, g_ref, o_ref):
    e = _rms(_dot(p_ref[...].astype(BF16), wp_ref[...]), g_ref[...])
    gate = _sigmoid(_dot(h_ref[...], wg_ref[...]))
    o_ref[...] = x_ref[...] + gate * e


def _ple(x2, h3, p, w_gate, w_proj, g, tm):
    m = x2.shape[0]
    row = lambda i: (i, 0)
    return pl.pallas_call(
        _ple_kernel,
        grid=(m // tm,),
        in_specs=[
            pl.BlockSpec((tm, D_MODEL), row),
            pl.BlockSpec((tm, D_MODEL), row),
            pl.BlockSpec((tm, PLE_DIM), row),
            _resident((D_MODEL, D_MODEL)),
            _resident((PLE_DIM, D_MODEL)),
            _resident((1, D_MODEL)),
        ],
        out_specs=pl.BlockSpec((tm, D_MODEL), row),
        out_shape=jax.ShapeDtypeStruct((m, D_MODEL), F32),
        compiler_params=_params("parallel"),
        name="ple",
    )(x2, h3, p, w_gate, w_proj, g)


def _layer(x, p, norm_mix_g, w_in, gmlp_ln_g, gmlp_ln_b, gmlp_w_s, gmlp_b_s, fox_b_f, q_norm_g,
           k_norm_g, w_branch_a, w_branch_b, w_out, norm_ffn_g, w_up, conv_w, conv_b, w_down,
           ple_proj, ple_norm_g, ple_gate_norm_g, w_ple_gate, *, batch, seq):
    row_vec = lambda v: v.reshape(1, -1)
    w_in = w_in.astype(BF16)
    w_g = w_in[:, G_COLS:]
    b_f = jnp.pad(fox_b_f, (0, LANES - HEADS)).reshape(1, LANES)

    o_a, h = _gmlp(x, row_vec(norm_mix_g), w_in, row_vec(gmlp_ln_g), row_vec(gmlp_ln_b),
                   gmlp_w_s, gmlp_b_s.T, tm=512)
    q_aug, k_aug, v = _qkv(h, w_in, b_f, row_vec(q_norm_g), row_vec(k_norm_g), tm=512, seq=seq)
    o_b = _attention(q_aug, k_aug, v, batch, seq, tq=512)
    y = _merge(h, o_a, o_b, w_g, w_branch_a.astype(BF16), w_branch_b.astype(BF16), tm=1024, tn=512)
    x1, h2 = _outproj(x, y, w_out.astype(BF16), row_vec(norm_ffn_g), tm=512)
    hidden = _ffn_up(h2, w_up.astype(BF16), conv_w, row_vec(conv_b), seq=seq, tf=512)
    x2, h3 = _ffn_down(x1, hidden, w_down.astype(BF16), row_vec(ple_gate_norm_g), tm=256)
    return _ple(x2, h3, p, w_ple_gate.astype(BF16), ple_proj.astype(BF16), row_vec(ple_norm_g), tm=512)


def kernel(x, p, norm_mix_g, w_in, gmlp_ln_g, gmlp_ln_b, gmlp_w_s, gmlp_b_s, fox_b_f, q_norm_g, k_norm_g, w_branch_a, w_branch_b, w_out, norm_ffn_g, w_up, conv_w, conv_b, w_down, ple_proj, ple_norm_g, ple_gate_norm_g, w_ple_gate):
    batch, seq, d = x.shape
    depth = w_in.shape[0]
    xs = x.reshape(batch * seq, d)
    for i in range(depth):
        xs = _layer(xs, p[i].reshape(batch * seq, -1), norm_mix_g[i], w_in[i], gmlp_ln_g[i], gmlp_ln_b[i],
                    gmlp_w_s[i], gmlp_b_s[i], fox_b_f[i], q_norm_g[i], k_norm_g[i], w_branch_a[i],
                    w_branch_b[i], w_out[i], norm_ffn_g[i], w_up[i], conv_w[i], conv_b[i], w_down[i],
                    ple_proj[i], ple_norm_g[i], ple_gate_norm_g[i], w_ple_gate[i], batch=batch, seq=seq)
    return xs.reshape(batch, seq, d)
```

```python
import functools
import math
from typing import NamedTuple

import jax
import jax.numpy as jnp
import numpy as np
from jax import lax
from jax.experimental import pallas as pl
from jax.experimental.pallas import tpu as pltpu

D_MODEL = 2048
CHUNK = 128
GROUPS = 8
GROUP_DIM = 128
GMLP_WIDTH = GROUPS * GROUP_DIM
HEADS = 8
HEAD_DIM = 128
ATTN_WIDTH = HEADS * HEAD_DIM
D_FF = 5632
PLE_DIM = 256
EPS = 1e-6
Q_COLS = 2 * GMLP_WIDTH
F_COLS = Q_COLS + 3 * ATTN_WIDTH
G_COLS = F_COLS + HEADS
LANES = 128
AUG = 2 * HEAD_DIM
N_SPLIT = 3
LOG2_E = 1.4426950408889634
DOT_ROWS = 512

VMEM_LIMIT = 56 * 1024 * 1024

F32 = jnp.float32
BF16 = jnp.bfloat16
NT = (((1,), (1,)), ((), ()))


def _params(*semantics):
    return pltpu.CompilerParams(dimension_semantics=semantics, vmem_limit_bytes=VMEM_LIMIT)


def _resident(shape, block=None):
    index = tuple(block) if block is not None else (0,) * len(shape)
    return pl.BlockSpec(shape, lambda *_: index, pipeline_mode=pl.Buffered(1))


def _dot(a, b):
    return jnp.dot(a, b, preferred_element_type=F32)


def _dot_nt(a, b):
    return lax.dot_general(a, b, NT, preferred_element_type=F32)


def _rms(xf, g):
    return xf * lax.rsqrt(jnp.mean(xf * xf, axis=-1, keepdims=True) + EPS) * g


def _gelu_erf(x):
    return 0.5 * x * (1.0 + lax.erf(x * (2.0 ** -0.5)))


def _gelu_tanh(x):
    return 0.5 * x * (1.0 + jnp.tanh((2.0 / jnp.pi) ** 0.5 * (x + 0.044715 * (x * x * x))))


def _sigmoid(x):
    return 1.0 / (1.0 + jnp.exp(-x))


def _log_sigmoid(x):
    return -(jnp.maximum(-x, 0.0) + jnp.log1p(jnp.exp(-jnp.abs(x))))


class CastJob(NamedTuple):
    src: jax.Array
    row_start: int
    rows: int


def _hosted_call(kernel_fn, *, grid, in_specs, out_specs, out_shape, semantics, name, casts=(),
                 scratch_shapes=()):
    n_in, n_out, n_cast = len(in_specs), len(out_specs), len(casts)
    n_steps = math.prod(grid)

    def step_of(*ids):
        step = ids[0]
        for extent, idx in zip(grid[1:], ids[1:]):
            step = step * extent + idx
        return step

    cast_in, cast_out, cast_shape = [], [], []
    for job in casts:
        slab, rem = divmod(job.rows, n_steps)
        assert rem == 0 and slab % 16 == 0 and job.row_start % 8 == 0, (name, job.rows, job.row_start, n_steps)
        cols = job.src.shape[1]
        cast_in.append(pl.BlockSpec(
            (pl.Element(slab), pl.Element(cols)),
            lambda *ids, start=job.row_start, slab=slab: (pl.multiple_of(start + step_of(*ids) * slab, 8), 0)))
        cast_out.append(pl.BlockSpec((slab, cols), lambda *ids: (step_of(*ids), 0)))
        cast_shape.append(jax.ShapeDtypeStruct((job.rows, cols), BF16))

    def body(*refs):
        ins, refs = refs[:n_in], refs[n_in:]
        srcs, refs = refs[:n_cast], refs[n_cast:]
        outs, refs = refs[:n_out], refs[n_out:]
        dsts, scratch = refs[:n_cast], refs[n_cast:]
        for src, dst in zip(srcs, dsts):
            dst[...] = src[...].astype(BF16)
        kernel_fn(*ins, *outs, *scratch)

    call = pl.pallas_call(
        body,
        grid=grid,
        in_specs=list(in_specs) + cast_in,
        out_specs=list(out_specs) + cast_out,
        out_shape=list(out_shape) + cast_shape,
        scratch_shapes=scratch_shapes,
        compiler_params=_params(*semantics),
        name=name,
    )

    def run(*operands):
        results = call(*operands, *(job.src for job in casts))
        return results[:n_out], results[n_out:]

    return run


def _gmlp_kernel(x_ref, g_ref, w_ref, lng_ref, lnb_ref, ws_ref, bst_ref, oa_ref, h_ref):
    tm = x_ref.shape[0]
    h = _rms(x_ref[...], g_ref[...]).astype(BF16)
    h_ref[...] = h
    z = _dot_nt(h, w_ref[...])
    u = _gelu_erf(z[:, :GMLP_WIDTH])
    v = _gelu_erf(z[:, GMLP_WIDTH:])
    mu = jnp.mean(v, axis=-1, keepdims=True)
    vc = v - mu
    vn = vc * lax.rsqrt(jnp.mean(vc * vc, axis=-1, keepdims=True) + EPS)
    vn = (vn * lng_ref[...] + lnb_ref[...]).astype(BF16)
    row = lax.broadcasted_iota(jnp.int32, (CHUNK, CHUNK), 0)
    col = lax.broadcasted_iota(jnp.int32, (CHUNK, CHUNK), 1)
    causal = row >= col
    for grp in range(GROUPS):
        cols = slice(grp * GROUP_DIM, (grp + 1) * GROUP_DIM)
        w = jnp.where(causal, ws_ref[grp], 0.0).astype(BF16)
        bias = bst_ref[:, grp:grp + 1]
        for c in range(tm // CHUNK):
            rows = slice(c * CHUNK, (c + 1) * CHUNK)
            mixed = _dot(w, vn[rows, cols]) + bias
            oa_ref[rows, cols] = (u[rows, cols] * mixed).astype(BF16)


def _gmlp(x, g, w_uv_t, ln_g, ln_b, w_s, b_s_t, tm, casts):
    m = x.shape[0]
    row = lambda i: (i, 0)
    (o_a, h), cast = _hosted_call(
        _gmlp_kernel,
        grid=(m // tm,),
        in_specs=[
            pl.BlockSpec((tm, D_MODEL), row),
            _resident((1, D_MODEL)),
            _resident((2 * GMLP_WIDTH, D_MODEL)),
            _resident((1, GMLP_WIDTH)),
            _resident((1, GMLP_WIDTH)),
            _resident((GROUPS, CHUNK, CHUNK)),
            _resident((CHUNK, GROUPS)),
        ],
        out_specs=[pl.BlockSpec((tm, GMLP_WIDTH), row), pl.BlockSpec((tm, D_MODEL), row)],
        out_shape=[jax.ShapeDtypeStruct((m, GMLP_WIDTH), BF16), jax.ShapeDtypeStruct((m, D_MODEL), BF16)],
        semantics=("parallel",),
        name="gmlp",
        casts=casts,
    )(x, g, w_uv_t, ln_g, ln_b, w_s, b_s_t)
    return o_a, h, cast


def _split_bf16(x):
    pieces = []
    for _ in range(N_SPLIT):
        piece = x.astype(BF16)
        pieces.append(piece)
        x = x - piece.astype(F32)
    return pieces


def _decay_constants(tm):
    lower = np.tril(np.ones((tm, tm), np.float32))
    sel_q = np.zeros((N_SPLIT * LANES, HEADS * HEAD_DIM), np.float32)
    sel_k = np.zeros_like(sel_q)
    ones_q = np.zeros((1, HEAD_DIM), np.float32)
    ones_k = np.zeros((1, HEAD_DIM), np.float32)
    for piece in range(N_SPLIT):
        ones_q[0, N_SPLIT + piece] = 1.0
        ones_k[0, piece] = 1.0
        for head in range(HEADS):
            sel_q[piece * LANES + head, head * HEAD_DIM + piece] = 1.0
            sel_k[piece * LANES + head, head * HEAD_DIM + N_SPLIT + piece] = -1.0
    as_bf16 = lambda a: jnp.asarray(a, BF16)
    return as_bf16(lower), as_bf16(sel_q), as_bf16(sel_k), jnp.asarray(ones_q), jnp.asarray(ones_k)


def _qkv_kernel(h_ref, wq_ref, wk_ref, wv_ref, wf_ref, bf_ref, qg_ref, kg_ref, lower_ref, selq_ref, selk_ref,
                oneq_ref, onek_ref, qa_ref, ka_ref, v_ref, carry_ref, *, tiles_per_seq):
    tm = h_ref.shape[0]
    h = h_ref[...]

    logf = _log_sigmoid(_dot_nt(h, wf_ref[...]) + bf_ref[...])
    lower = lower_ref[...]
    local = sum(_dot(lower, piece) for piece in _split_bf16(logf))

    @pl.when(pl.program_id(0) % tiles_per_seq == 0)
    def _():
        carry_ref[...] = jnp.zeros_like(carry_ref)

    cum = local + carry_ref[...]
    carry_ref[...] = cum[tm - 1:tm, :]
    pieces = jnp.concatenate(_split_bf16(cum * (HEAD_DIM ** 0.5)), axis=1)

    for z, gain_ref, sel_ref, ones_ref, out_ref in (
            (_dot_nt(h, wq_ref[...]), qg_ref, selq_ref, oneq_ref, qa_ref),
            (_dot_nt(h, wk_ref[...]), kg_ref, selk_ref, onek_ref, ka_ref)):
        decay = _dot(pieces, sel_ref[...])
        for hd in range(HEADS):
            cols = slice(hd * HEAD_DIM, (hd + 1) * HEAD_DIM)
            out_ref[:, hd * AUG:hd * AUG + HEAD_DIM] = _rms(z[:, cols], gain_ref[...]).astype(BF16)
            out_ref[:, hd * AUG + HEAD_DIM:(hd + 1) * AUG] = (decay[:, cols] + ones_ref[...]).astype(BF16)
    v_ref[...] = _dot_nt(h, wv_ref[...]).astype(BF16)


def _qkv(h, w_qkv_t, w_f_t, b_f, q_g, k_g, tm, seq, casts):
    m = h.shape[0]
    row = lambda i: (i, 0)
    consts = _decay_constants(tm)
    (q_aug, k_aug, v), cast = _hosted_call(
        functools.partial(_qkv_kernel, tiles_per_seq=seq // tm),
        grid=(m // tm,),
        in_specs=[
            pl.BlockSpec((tm, D_MODEL), row),
            _resident((ATTN_WIDTH, D_MODEL), (0, 0)),
            _resident((ATTN_WIDTH, D_MODEL), (1, 0)),
            _resident((ATTN_WIDTH, D_MODEL), (2, 0)),
            _resident((LANES, D_MODEL)),
            _resident((1, LANES)),
            _resident((1, HEAD_DIM)),
            _resident((1, HEAD_DIM)),
        ] + [_resident(c.shape) for c in consts],
        out_specs=[
            pl.BlockSpec((tm, HEADS * AUG), row),
            pl.BlockSpec((tm, HEADS * AUG), row),
            pl.BlockSpec((tm, ATTN_WIDTH), row),
        ],
        out_shape=[
            jax.ShapeDtypeStruct((m, HEADS * AUG), BF16),
            jax.ShapeDtypeStruct((m, HEADS * AUG), BF16),
            jax.ShapeDtypeStruct((m, ATTN_WIDTH), BF16),
        ],
        scratch_shapes=[pltpu.VMEM((1, LANES), F32)],
        semantics=("arbitrary",),
        name="qkv",
        casts=casts,
    )(h, w_qkv_t, w_qkv_t, w_qkv_t, w_f_t, b_f, q_g, k_g, *consts)
    return q_aug, k_aug, v, cast


def _attn_kernel(q_ref, k_ref, v_ref, o_ref, *, tq):
    seq = q_ref.shape[0]
    to_log2 = (HEAD_DIM ** -0.5) * LOG2_E
    row = lax.broadcasted_iota(jnp.int32, (tq, tq), 0)
    col = lax.broadcasted_iota(jnp.int32, (tq, tq), 1)
    keep = row >= col
    v_aug = jnp.concatenate([v_ref[...], jnp.ones((seq, HEAD_DIM), BF16)], axis=1)
    for qi in range(seq // tq):
        q0, q1 = qi * tq, (qi + 1) * tq
        q = q_ref[q0:q1, :]
        diag = jnp.where(keep, _dot_nt(q, k_ref[q0:q1, :]), -jnp.inf)
        m = jnp.max(diag, axis=-1, keepdims=True)
        if qi:
            past = _dot_nt(q, k_ref[:q0, :])
            m = jnp.maximum(m, jnp.max(past, axis=-1, keepdims=True))
        acc = _dot(jnp.exp2((diag - m) * to_log2).astype(BF16), v_aug[q0:q1, :])
        if qi:
            acc = acc + _dot(jnp.exp2((past - m) * to_log2).astype(BF16), v_aug[:q0, :])
        o_ref[q0:q1, :] = (acc[:, :HEAD_DIM] / acc[:, HEAD_DIM:]).astype(BF16)


def _attention(q_aug, k_aug, v, batch, seq, tq, casts):
    m = v.shape[0]
    head_block = lambda b, hd: (b, hd)
    (o_b,), cast = _hosted_call(
        functools.partial(_attn_kernel, tq=tq),
        grid=(batch, HEADS),
        in_specs=[
            pl.BlockSpec((seq, AUG), head_block),
            pl.BlockSpec((seq, AUG), head_block),
            pl.BlockSpec((seq, HEAD_DIM), head_block),
        ],
        out_specs=[pl.BlockSpec((seq, HEAD_DIM), head_block)],
        out_shape=[jax.ShapeDtypeStruct((m, ATTN_WIDTH), BF16)],
        semantics=("parallel", "parallel"),
        name="attention",
        casts=casts,
    )(q_aug, k_aug, v)
    return o_b, cast


def _merge_kernel(h_ref, oa_ref, ob_ref, wga_ref, wgb_ref, wba_ref, wbb_ref, y_ref):
    h = h_ref[...]
    ya = _sigmoid(_dot_nt(h, wga_ref[...])) * _dot(oa_ref[...], wba_ref[...])
    yb = _sigmoid(_dot_nt(h, wgb_ref[...])) * _dot(ob_ref[...], wbb_ref[...])
    y_ref[...] = (ya + yb).astype(BF16)


def _merge(h, o_a, o_b, w_g_t, w_ba, w_bb, tm, tn, casts):
    m = h.shape[0]
    nn = D_MODEL // tn
    row = lambda i, j: (i, 0)
    col = lambda i, j: (0, j)
    (y,), cast = _hosted_call(
        _merge_kernel,
        grid=(m // tm, nn),
        in_specs=[
            pl.BlockSpec((tm, D_MODEL), row),
            pl.BlockSpec((tm, GMLP_WIDTH), row),
            pl.BlockSpec((tm, ATTN_WIDTH), row),
            pl.BlockSpec((tn, D_MODEL), lambda i, j: (j, 0)),
            pl.BlockSpec((tn, D_MODEL), lambda i, j: (nn + j, 0)),
            pl.BlockSpec((GMLP_WIDTH, tn), col),
            pl.BlockSpec((ATTN_WIDTH, tn), col),
        ],
        out_specs=[pl.BlockSpec((tm, tn), lambda i, j: (i, j))],
        out_shape=[jax.ShapeDtypeStruct((m, D_MODEL), BF16)],
        semantics=("parallel", "arbitrary"),
        name="merge",
        casts=casts,
    )(h, o_a, o_b, w_g_t, w_g_t, w_ba, w_bb)
    return y, cast


def _outproj_kernel(x_ref, y_ref, w_ref, g_ref, x1_ref, h2_ref):
    x1 = x_ref[...] + _dot(y_ref[...], w_ref[...])
    x1_ref[...] = x1
    h2_ref[...] = _rms(x1, g_ref[...]).astype(BF16)


def _outproj(x, y, w_out, g_next, tm):
    m = x.shape[0]
    row = lambda i: (i, 0)
    return pl.pallas_call(
        _outproj_kernel,
        grid=(m // tm,),
        in_specs=[
            pl.BlockSpec((tm, D_MODEL), row),
            pl.BlockSpec((tm, D_MODEL), row),
            _resident((D_MODEL, D_MODEL)),
            _resident((1, D_MODEL)),
        ],
        out_specs=[pl.BlockSpec((tm, D_MODEL), row), pl.BlockSpec((tm, D_MODEL), row)],
        out_shape=[jax.ShapeDtypeStruct((m, D_MODEL), F32), jax.ShapeDtypeStruct((m, D_MODEL), BF16)],
        compiler_params=_params("parallel"),
        name="outproj",
    )(x, y, w_out, g_next)


def _ffn_up_kernel(h_ref, wa_ref, wb_ref, cw_ref, cb_ref, o_ref):
    seq = h_ref.shape[0]

    def proj(w):
        return jnp.concatenate([_dot(h_ref[r:r + DOT_ROWS, :], w) for r in range(0, seq, DOT_ROWS)], axis=0)

    a = proj(wa_ref[...])
    t = lax.broadcasted_iota(jnp.int32, a.shape, 0)
    a1 = jnp.where(t >= 1, pltpu.roll(a, 1, 0), 0.0)
    a2 = jnp.where(t >= 2, pltpu.roll(a, 2, 0), 0.0)
    conv = cb_ref[...] + a2 * cw_ref[0:1, :] + a1 * cw_ref[1:2, :] + a * cw_ref[2:3, :]
    o_ref[...] = (_gelu_tanh(conv) * proj(wb_ref[...])).astype(BF16)


def _ffn_up(h2, w_up, conv_w, conv_b, seq, tf):
    m = h2.shape[0]
    nf = D_FF // tf
    col = lambda b, j: (0, j)
    return pl.pallas_call(
        _ffn_up_kernel,
        grid=(m // seq, nf),
        in_specs=[
            pl.BlockSpec((seq, D_MODEL), lambda b, j: (b, 0)),
            pl.BlockSpec((D_MODEL, tf), col),
            pl.BlockSpec((D_MODEL, tf), lambda b, j: (0, nf + j)),
            pl.BlockSpec((conv_w.shape[0], tf), col),
            pl.BlockSpec((1, tf), col),
        ],
        out_specs=pl.BlockSpec((seq, tf), lambda b, j: (b, j)),
        out_shape=jax.ShapeDtypeStruct((m, D_FF), BF16),
        compiler_params=_params("parallel", "arbitrary"),
        name="ffn_up",
    )(h2, w_up, w_up, conv_w, conv_b)


def _ffn_down_kernel(x_ref, hid_ref, w_ref, g_ref, x2_ref, h3_ref):
    x2 = x_ref[...] + _dot(hid_ref[...], w_ref[...])
    x2_ref[...] = x2
    h3_ref[...] = _rms(x2, g_ref[...]).astype(BF16)


def _ffn_down(x1, hidden, w_down, g_next, tm):
    m = x1.shape[0]
    row = lambda i: (i, 0)
    return pl.pallas_call(
        _ffn_down_kernel,
        grid=(m // tm,),
        in_specs=[
            pl.BlockSpec((tm, D_MODEL), row),
            pl.BlockSpec((tm, D_FF), row),
            _resident((D_FF, D_MODEL)),
            _resident((1, D_MODEL)),
        ],
        out_specs=[pl.BlockSpec((tm, D_MODEL), row), pl.BlockSpec((tm, D_MODEL), row)],
        out_shape=[jax.ShapeDtypeStruct((m, D_MODEL), F32), jax.ShapeDtypeStruct((m, D_MODEL), BF16)],
        compiler_params=_params("parallel"),
        name="ffn_down",
    )(x1, hidden, w_down, g_next)


def _ple_kernel(x_ref, h_ref, p_ref, wg_ref, wp_ref, g_ref, o_ref):
    e = _rms(_dot(p_ref[...].astype(BF16), wp_ref[...].astype(BF16)), g_ref[...])
    gate = _sigmoid(_dot(h_ref[...], wg_ref[...]))
    o_ref[...] = x_ref[...] + gate * e


def _ple(x2, h3, p, w_gate, w_proj, g, tm):
    m = x2.shape[0]
    row = lambda i: (i, 0)
    return pl.pallas_call(
        _ple_kernel,
        grid=(m // tm,),
        in_specs=[
            pl.BlockSpec((tm, D_MODEL), row),
            pl.BlockSpec((tm, D_MODEL), row),
            pl.BlockSpec((tm, PLE_DIM), row),
            _resident((D_MODEL, D_MODEL)),
            _resident((PLE_DIM, D_MODEL)),
            _resident((1, D_MODEL)),
        ],
        out_specs=pl.BlockSpec((tm, D_MODEL), row),
        out_shape=jax.ShapeDtypeStruct((m, D_MODEL), F32),
        compiler_params=_params("parallel"),
        name="ple",
    )(x2, h3, p, w_gate, w_proj, g)


def _layer(x, p, norm_mix_g, w_in, gmlp_ln_g, gmlp_ln_b, gmlp_w_s, gmlp_b_s, fox_b_f, q_norm_g,
           k_norm_g, w_branch_a, w_branch_b, w_out, norm_ffn_g, w_up, conv_w, conv_b, w_down,
           ple_proj, ple_norm_g, ple_gate_norm_g, w_ple_gate, *, batch, seq):
    row_vec = lambda v: v.reshape(1, -1)
    whole = lambda w: CastJob(w, 0, w.shape[0])
    w_in_t = w_in.T
    w_uv_t = w_in_t[:Q_COLS].astype(BF16)
    w_f_t = jnp.pad(w_in_t[F_COLS:G_COLS], ((0, LANES - HEADS), (0, 0))).astype(BF16)
    b_f = jnp.pad(fox_b_f, (0, LANES - HEADS)).reshape(1, LANES)

    o_a, h, (w_qkv_t, w_down_bf) = _gmlp(
        x, row_vec(norm_mix_g), w_uv_t, row_vec(gmlp_ln_g), row_vec(gmlp_ln_b), gmlp_w_s, gmlp_b_s.T, tm=512,
        casts=[CastJob(w_in_t, Q_COLS, 3 * ATTN_WIDTH), whole(w_down)])
    q_aug, k_aug, v, (w_g_t,) = _qkv(
        h, w_qkv_t, w_f_t, b_f, row_vec(q_norm_g), row_vec(k_norm_g), tm=512, seq=seq,
        casts=[CastJob(w_in_t, G_COLS, 2 * D_MODEL)])
    o_b, (w_up_bf, w_ba_bf, w_bb_bf) = _attention(
        q_aug, k_aug, v, batch, seq, tq=512, casts=[whole(w_up), whole(w_branch_a), whole(w_branch_b)])
    y, (w_out_bf, w_gate_bf) = _merge(
        h, o_a, o_b, w_g_t, w_ba_bf, w_bb_bf, tm=1024, tn=512, casts=[whole(w_out), whole(w_ple_gate)])
    x1, h2 = _outproj(x, y, w_out_bf, row_vec(norm_ffn_g), tm=512)
    hidden = _ffn_up(h2, w_up_bf, conv_w, row_vec(conv_b), seq=seq, tf=512)
    x2, h3 = _ffn_down(x1, hidden, w_down_bf, row_vec(ple_gate_norm_g), tm=256)
    return _ple(x2, h3, p, w_gate_bf, ple_proj, row_vec(ple_norm_g), tm=512)


def kernel(x, p, norm_mix_g, w_in, gmlp_ln_g, gmlp_ln_b, gmlp_w_s, gmlp_b_s, fox_b_f, q_norm_g, k_norm_g, w_branch_a, w_branch_b, w_out, norm_ffn_g, w_up, conv_w, conv_b, w_down, ple_proj, ple_norm_g, ple_gate_norm_g, w_ple_gate):
    batch, seq, d = x.shape
    depth = w_in.shape[0]
    xs = x.reshape(batch * seq, d)
    for i in range(depth):
        xs = _layer(xs, p[i].reshape(batch * seq, -1), norm_mix_g[i], w_in[i], gmlp_ln_g[i], gmlp_ln_b[i],
                    gmlp_w_s[i], gmlp_b_s[i], fox_b_f[i], q_norm_g[i], k_norm_g[i], w_branch_a[i],
                    w_branch_b[i], w_out[i], norm_ffn_g[i], w_up[i], conv_w[i], conv_b[i], w_down[i],
                    ple_proj[i], ple_norm_g[i], ple_gate_norm_g[i], w_ple_gate[i], batch=batch, seq=seq)
    return xs.reshape(batch, seq, d)
```

```python
import functools
import math
from typing import NamedTuple

import jax
import jax.numpy as jnp
import numpy as np
from jax import lax
from jax.experimental import pallas as pl
from jax.experimental.pallas import tpu as pltpu

D_MODEL = 2048
CHUNK = 128
GROUPS = 8
GROUP_DIM = 128
GMLP_WIDTH = GROUPS * GROUP_DIM
HEADS = 8
HEAD_DIM = 128
ATTN_WIDTH = HEADS * HEAD_DIM
D_FF = 5632
PLE_DIM = 256
EPS = 1e-6
Q_COLS = 2 * GMLP_WIDTH
F_COLS = Q_COLS + 3 * ATTN_WIDTH
G_COLS = F_COLS + HEADS
LANES = 128
AUG = 2 * HEAD_DIM
N_SPLIT = 3
LOG2_E = 1.4426950408889634
DOT_ROWS = 512
SUB_ROWS = 256

VMEM_LIMIT = 56 * 1024 * 1024

F32 = jnp.float32
BF16 = jnp.bfloat16
NT = (((1,), (1,)), ((), ()))


def _params(*semantics):
    return pltpu.CompilerParams(dimension_semantics=semantics, vmem_limit_bytes=VMEM_LIMIT)


def _resident(shape, block=None):
    index = tuple(block) if block is not None else (0,) * len(shape)
    return pl.BlockSpec(shape, lambda *_: index, pipeline_mode=pl.Buffered(1))


def _dot(a, b):
    return jnp.dot(a, b, preferred_element_type=F32)


def _dot_nt(a, b):
    return lax.dot_general(a, b, NT, preferred_element_type=F32)


def _software_pipeline(items, produce, consume):
    pending = None
    for item in items:
        produced = produce(item)
        if pending is not None:
            consume(*pending)
        pending = (item, produced)
    consume(*pending)


def _rms(xf, g):
    return xf * lax.rsqrt(jnp.mean(xf * xf, axis=-1, keepdims=True) + EPS) * g


def _gelu_erf(x):
    return 0.5 * x * (1.0 + lax.erf(x * (2.0 ** -0.5)))


def _gelu_tanh(x):
    return 0.5 * x * (1.0 + jnp.tanh((2.0 / jnp.pi) ** 0.5 * (x + 0.044715 * (x * x * x))))


def _sigmoid(x):
    return 1.0 / (1.0 + jnp.exp(-x))


def _log_sigmoid(x):
    return -(jnp.maximum(-x, 0.0) + jnp.log1p(jnp.exp(-jnp.abs(x))))


class CastJob(NamedTuple):
    src: jax.Array
    row_start: int
    rows: int


def _hosted_call(kernel_fn, *, grid, in_specs, out_specs, out_shape, semantics, name, casts=(),
                 scratch_shapes=()):
    n_in, n_out, n_cast = len(in_specs), len(out_specs), len(casts)
    n_steps = math.prod(grid)

    def step_of(*ids):
        step = ids[0]
        for extent, idx in zip(grid[1:], ids[1:]):
            step = step * extent + idx
        return step

    cast_in, cast_out, cast_shape = [], [], []
    for job in casts:
        slab, rem = divmod(job.rows, n_steps)
        assert rem == 0 and slab % 16 == 0 and job.row_start % 8 == 0, (name, job.rows, job.row_start, n_steps)
        cols = job.src.shape[1]
        cast_in.append(pl.BlockSpec(
            (pl.Element(slab), pl.Element(cols)),
            lambda *ids, start=job.row_start, slab=slab: (pl.multiple_of(start + step_of(*ids) * slab, 8), 0)))
        cast_out.append(pl.BlockSpec((slab, cols), lambda *ids: (step_of(*ids), 0)))
        cast_shape.append(jax.ShapeDtypeStruct((job.rows, cols), BF16))

    def body(*refs):
        ins, refs = refs[:n_in], refs[n_in:]
        srcs, refs = refs[:n_cast], refs[n_cast:]
        outs, refs = refs[:n_out], refs[n_out:]
        dsts, scratch = refs[:n_cast], refs[n_cast:]
        for src, dst in zip(srcs, dsts):
            dst[...] = src[...].astype(BF16)
        kernel_fn(*ins, *outs, *scratch)

    call = pl.pallas_call(
        body,
        grid=grid,
        in_specs=list(in_specs) + cast_in,
        out_specs=list(out_specs) + cast_out,
        out_shape=list(out_shape) + cast_shape,
        scratch_shapes=scratch_shapes,
        compiler_params=_params(*semantics),
        name=name,
    )

    def run(*operands):
        results = call(*operands, *(job.src for job in casts))
        return results[:n_out], results[n_out:]

    return run


def _gmlp_kernel(x_ref, g_ref, w_ref, lng_ref, lnb_ref, ws_ref, bst_ref, oa_ref, h_ref):
    tm = x_ref.shape[0]
    row = lax.broadcasted_iota(jnp.int32, (CHUNK, CHUNK), 0)
    col = lax.broadcasted_iota(jnp.int32, (CHUNK, CHUNK), 1)
    causal = row >= col
    w_mix = [jnp.where(causal, ws_ref[grp], 0.0).astype(BF16) for grp in range(GROUPS)]

    def project(r0):
        sub = slice(r0, r0 + SUB_ROWS)
        h = _rms(x_ref[sub, :], g_ref[...]).astype(BF16)
        h_ref[sub, :] = h
        return _dot_nt(h, w_ref[...])

    def mix(r0, z):
        u = _gelu_erf(z[:, :GMLP_WIDTH])
        v = _gelu_erf(z[:, GMLP_WIDTH:])
        mu = jnp.mean(v, axis=-1, keepdims=True)
        vc = v - mu
        vn = vc * lax.rsqrt(jnp.mean(vc * vc, axis=-1, keepdims=True) + EPS)
        vn = (vn * lng_ref[...] + lnb_ref[...]).astype(BF16)
        for grp in range(GROUPS):
            cols = slice(grp * GROUP_DIM, (grp + 1) * GROUP_DIM)
            bias = bst_ref[:, grp:grp + 1]
            for c0 in range(0, SUB_ROWS, CHUNK):
                rows = slice(c0, c0 + CHUNK)
                mixed = _dot(w_mix[grp], vn[rows, cols]) + bias
                oa_ref[r0 + c0:r0 + c0 + CHUNK, cols] = (u[rows, cols] * mixed).astype(BF16)

    _software_pipeline(range(0, tm, SUB_ROWS), project, mix)


def _gmlp(x, g, w_uv_t, ln_g, ln_b, w_s, b_s_t, tm, casts):
    m = x.shape[0]
    row = lambda i: (i, 0)
    (o_a, h), cast = _hosted_call(
        _gmlp_kernel,
        grid=(m // tm,),
        in_specs=[
            pl.BlockSpec((tm, D_MODEL), row),
            _resident((1, D_MODEL)),
            _resident((2 * GMLP_WIDTH, D_MODEL)),
            _resident((1, GMLP_WIDTH)),
            _resident((1, GMLP_WIDTH)),
            _resident((GROUPS, CHUNK, CHUNK)),
            _resident((CHUNK, GROUPS)),
        ],
        out_specs=[pl.BlockSpec((tm, GMLP_WIDTH), row), pl.BlockSpec((tm, D_MODEL), row)],
        out_shape=[jax.ShapeDtypeStruct((m, GMLP_WIDTH), BF16), jax.ShapeDtypeStruct((m, D_MODEL), BF16)],
        semantics=("parallel",),
        name="gmlp",
        casts=casts,
    )(x, g, w_uv_t, ln_g, ln_b, w_s, b_s_t)
    return o_a, h, cast


def _split_bf16(x):
    pieces = []
    for _ in range(N_SPLIT):
        piece = x.astype(BF16)
        pieces.append(piece)
        x = x - piece.astype(F32)
    return pieces


def _decay_constants(tm):
    lower = np.tril(np.ones((tm, tm), np.float32))
    sel_q = np.zeros((N_SPLIT * LANES, HEADS * HEAD_DIM), np.float32)
    sel_k = np.zeros_like(sel_q)
    ones_q = np.zeros((1, HEAD_DIM), np.float32)
    ones_k = np.zeros((1, HEAD_DIM), np.float32)
    for piece in range(N_SPLIT):
        ones_q[0, N_SPLIT + piece] = 1.0
        ones_k[0, piece] = 1.0
        for head in range(HEADS):
            sel_q[piece * LANES + head, head * HEAD_DIM + piece] = 1.0
            sel_k[piece * LANES + head, head * HEAD_DIM + N_SPLIT + piece] = -1.0
    as_bf16 = lambda a: jnp.asarray(a, BF16)
    return as_bf16(lower), as_bf16(sel_q), as_bf16(sel_k), jnp.asarray(ones_q), jnp.asarray(ones_k)


def _qkv_kernel(h_ref, wq_ref, wk_ref, wv_ref, wf_ref, bf_ref, qg_ref, kg_ref, lower_ref, selq_ref, selk_ref,
                oneq_ref, onek_ref, qa_ref, ka_ref, v_ref, carry_ref, *, tiles_per_seq):
    tm = h_ref.shape[0]
    h = h_ref[...]

    @pl.when(pl.program_id(0) % tiles_per_seq == 0)
    def _():
        carry_ref[...] = jnp.zeros_like(carry_ref)

    f_logit = _dot_nt(h, wf_ref[...]) + bf_ref[...]
    z_q = _dot_nt(h, wq_ref[...])
    lower = lower_ref[...]
    local = sum(_dot(lower, piece) for piece in _split_bf16(_log_sigmoid(f_logit)))
    z_k = _dot_nt(h, wk_ref[...])
    cum = local + carry_ref[...]
    carry_ref[...] = cum[tm - 1:tm, :]
    pieces = jnp.concatenate(_split_bf16(cum * (HEAD_DIM ** 0.5)), axis=1)
    decay_q = _dot(pieces, selq_ref[...])
    decay_k = _dot(pieces, selk_ref[...])
    z_v = _dot_nt(h, wv_ref[...])

    for z, decay, gain_ref, ones_ref, out_ref in ((z_q, decay_q, qg_ref, oneq_ref, qa_ref),
                                                  (z_k, decay_k, kg_ref, onek_ref, ka_ref)):
        for hd in range(HEADS):
            cols = slice(hd * HEAD_DIM, (hd + 1) * HEAD_DIM)
            out_ref[:, hd * AUG:hd * AUG + HEAD_DIM] = _rms(z[:, cols], gain_ref[...]).astype(BF16)
            out_ref[:, hd * AUG + HEAD_DIM:(hd + 1) * AUG] = (decay[:, cols] + ones_ref[...]).astype(BF16)
    v_ref[...] = z_v.astype(BF16)


def _qkv(h, w_qkv_t, w_f_t, b_f, q_g, k_g, tm, seq, casts):
    m = h.shape[0]
    row = lambda i: (i, 0)
    consts = _decay_constants(tm)
    (q_aug, k_aug, v), cast = _hosted_call(
        functools.partial(_qkv_kernel, tiles_per_seq=seq // tm),
        grid=(m // tm,),
        in_specs=[
            pl.BlockSpec((tm, D_MODEL), row),
            _resident((ATTN_WIDTH, D_MODEL), (0, 0)),
            _resident((ATTN_WIDTH, D_MODEL), (1, 0)),
            _resident((ATTN_WIDTH, D_MODEL), (2, 0)),
            _resident((LANES, D_MODEL)),
            _resident((1, LANES)),
            _resident((1, HEAD_DIM)),
            _resident((1, HEAD_DIM)),
        ] + [_resident(c.shape) for c in consts],
        out_specs=[
            pl.BlockSpec((tm, HEADS * AUG), row),
            pl.BlockSpec((tm, HEADS * AUG), row),
            pl.BlockSpec((tm, ATTN_WIDTH), row),
        ],
        out_shape=[
            jax.ShapeDtypeStruct((m, HEADS * AUG), BF16),
            jax.ShapeDtypeStruct((m, HEADS * AUG), BF16),
            jax.ShapeDtypeStruct((m, ATTN_WIDTH), BF16),
        ],
        scratch_shapes=[pltpu.VMEM((1, LANES), F32)],
        semantics=("arbitrary",),
        name="qkv",
        casts=casts,
    )(h, w_qkv_t, w_qkv_t, w_qkv_t, w_f_t, b_f, q_g, k_g, *consts)
    return q_aug, k_aug, v, cast


def _attn_kernel(q_ref, k_ref, v_ref, o_ref, *, tq):
    seq = q_ref.shape[0]
    to_log2 = (HEAD_DIM ** -0.5) * LOG2_E
    row = lax.broadcasted_iota(jnp.int32, (tq, tq), 0)
    col = lax.broadcasted_iota(jnp.int32, (tq, tq), 1)
    keep = row >= col
    v_aug = jnp.concatenate([v_ref[...], jnp.ones((seq, HEAD_DIM), BF16)], axis=1)
    for qi in reversed(range(seq // tq)):
        q0, q1 = qi * tq, (qi + 1) * tq
        q = q_ref[q0:q1, :]
        diag = jnp.where(keep, _dot_nt(q, k_ref[q0:q1, :]), -jnp.inf)
        m = jnp.max(diag, axis=-1, keepdims=True)
        if qi:
            past = _dot_nt(q, k_ref[:q0, :])
            m = jnp.maximum(m, jnp.max(past, axis=-1, keepdims=True))
        acc = _dot(jnp.exp2((diag - m) * to_log2).astype(BF16), v_aug[q0:q1, :])
        if qi:
            acc = acc + _dot(jnp.exp2((past - m) * to_log2).astype(BF16), v_aug[:q0, :])
        o_ref[q0:q1, :] = (acc[:, :HEAD_DIM] / acc[:, HEAD_DIM:]).astype(BF16)


def _attention(q_aug, k_aug, v, batch, seq, tq, casts):
    m = v.shape[0]
    head_block = lambda b, hd: (b, hd)
    (o_b,), cast = _hosted_call(
        functools.partial(_attn_kernel, tq=tq),
        grid=(batch, HEADS),
        in_specs=[
            pl.BlockSpec((seq, AUG), head_block),
            pl.BlockSpec((seq, AUG), head_block),
            pl.BlockSpec((seq, HEAD_DIM), head_block),
        ],
        out_specs=[pl.BlockSpec((seq, HEAD_DIM), head_block)],
        out_shape=[jax.ShapeDtypeStruct((m, ATTN_WIDTH), BF16)],
        semantics=("parallel", "parallel"),
        name="attention",
        casts=casts,
    )(q_aug, k_aug, v)
    return o_b, cast


def _merge_kernel(h_ref, oa_ref, ob_ref, wga_ref, wgb_ref, wba_ref, wbb_ref, y_ref):
    h = h_ref[...]
    ya = _sigmoid(_dot_nt(h, wga_ref[...])) * _dot(oa_ref[...], wba_ref[...])
    yb = _sigmoid(_dot_nt(h, wgb_ref[...])) * _dot(ob_ref[...], wbb_ref[...])
    y_ref[...] = (ya + yb).astype(BF16)


def _merge(h, o_a, o_b, w_g_t, w_ba, w_bb, tm, tn, casts):
    m = h.shape[0]
    nn = D_MODEL // tn
    row = lambda i, j: (i, 0)
    col = lambda i, j: (0, j)
    (y,), cast = _hosted_call(
        _merge_kernel,
        grid=(m // tm, nn),
        in_specs=[
            pl.BlockSpec((tm, D_MODEL), row),
            pl.BlockSpec((tm, GMLP_WIDTH), row),
            pl.BlockSpec((tm, ATTN_WIDTH), row),
            pl.BlockSpec((tn, D_MODEL), lambda i, j: (j, 0)),
            pl.BlockSpec((tn, D_MODEL), lambda i, j: (nn + j, 0)),
            pl.BlockSpec((GMLP_WIDTH, tn), col),
            pl.BlockSpec((ATTN_WIDTH, tn), col),
        ],
        out_specs=[pl.BlockSpec((tm, tn), lambda i, j: (i, j))],
        out_shape=[jax.ShapeDtypeStruct((m, D_MODEL), BF16)],
        semantics=("parallel", "arbitrary"),
        name="merge",
        casts=casts,
    )(h, o_a, o_b, w_g_t, w_g_t, w_ba, w_bb)
    return y, cast


def _outproj_kernel(x_ref, y_ref, w_ref, g_ref, x1_ref, h2_ref):
    def project(r0):
        return _dot(y_ref[r0:r0 + SUB_ROWS, :], w_ref[...])

    def finish(r0, z):
        sub = slice(r0, r0 + SUB_ROWS)
        x1 = x_ref[sub, :] + z
        x1_ref[sub, :] = x1
        h2_ref[sub, :] = _rms(x1, g_ref[...]).astype(BF16)

    _software_pipeline(range(0, x_ref.shape[0], SUB_ROWS), project, finish)


def _outproj(x, y, w_out, g_next, tm):
    m = x.shape[0]
    row = lambda i: (i, 0)
    return pl.pallas_call(
        _outproj_kernel,
        grid=(m // tm,),
        in_specs=[
            pl.BlockSpec((tm, D_MODEL), row),
            pl.BlockSpec((tm, D_MODEL), row),
            _resident((D_MODEL, D_MODEL)),
            _resident((1, D_MODEL)),
        ],
        out_specs=[pl.BlockSpec((tm, D_MODEL), row), pl.BlockSpec((tm, D_MODEL), row)],
        out_shape=[jax.ShapeDtypeStruct((m, D_MODEL), F32), jax.ShapeDtypeStruct((m, D_MODEL), BF16)],
        compiler_params=_params("parallel"),
        name="outproj",
    )(x, y, w_out, g_next)


def _ffn_up_kernel(h_ref, wa_ref, wb_ref, cw_ref, cb_ref, o_ref):
    seq = h_ref.shape[0]

    def proj(w):
        return jnp.concatenate([_dot(h_ref[r:r + DOT_ROWS, :], w) for r in range(0, seq, DOT_ROWS)], axis=0)

    a = proj(wa_ref[...])
    t = lax.broadcasted_iota(jnp.int32, a.shape, 0)
    a1 = jnp.where(t >= 1, pltpu.roll(a, 1, 0), 0.0)
    a2 = jnp.where(t >= 2, pltpu.roll(a, 2, 0), 0.0)
    conv = cb_ref[...] + a2 * cw_ref[0:1, :] + a1 * cw_ref[1:2, :] + a * cw_ref[2:3, :]
    o_ref[...] = (_gelu_tanh(conv) * proj(wb_ref[...])).astype(BF16)


def _ffn_up(h2, w_up, conv_w, conv_b, seq, tf):
    m = h2.shape[0]
    nf = D_FF // tf
    col = lambda b, j: (0, j)
    return pl.pallas_call(
        _ffn_up_kernel,
        grid=(m // seq, nf),
        in_specs=[
            pl.BlockSpec((seq, D_MODEL), lambda b, j: (b, 0)),
            pl.BlockSpec((D_MODEL, tf), col),
            pl.BlockSpec((D_MODEL, tf), lambda b, j: (0, nf + j)),
            pl.BlockSpec((conv_w.shape[0], tf), col),
            pl.BlockSpec((1, tf), col),
        ],
        out_specs=pl.BlockSpec((seq, tf), lambda b, j: (b, j)),
        out_shape=jax.ShapeDtypeStruct((m, D_FF), BF16),
        compiler_params=_params("parallel", "arbitrary"),
        name="ffn_up",
    )(h2, w_up, w_up, conv_w, conv_b)


def _ffn_down_kernel(x_ref, hid_ref, w_ref, g_ref, x2_ref, h3_ref):
    x2 = x_ref[...] + _dot(hid_ref[...], w_ref[...])
    x2_ref[...] = x2
    h3_ref[...] = _rms(x2, g_ref[...]).astype(BF16)


def _ffn_down(x1, hidden, w_down, g_next, tm):
    m = x1.shape[0]
    row = lambda i: (i, 0)
    return pl.pallas_call(
        _ffn_down_kernel,
        grid=(m // tm,),
        in_specs=[
            pl.BlockSpec((tm, D_MODEL), row),
            pl.BlockSpec((tm, D_FF), row),
            _resident((D_FF, D_MODEL)),
            _resident((1, D_MODEL)),
        ],
        out_specs=[pl.BlockSpec((tm, D_MODEL), row), pl.BlockSpec((tm, D_MODEL), row)],
        out_shape=[jax.ShapeDtypeStruct((m, D_MODEL), F32), jax.ShapeDtypeStruct((m, D_MODEL), BF16)],
        compiler_params=_params("parallel"),
        name="ffn_down",
    )(x1, hidden, w_down, g_next)


def _ple_kernel(x_ref, h_ref, p_ref, wg_ref, wp_ref, g_ref, o_ref):
    w_proj = wp_ref[...].astype(BF16)

    def project(r0):
        sub = slice(r0, r0 + SUB_ROWS)
        return _dot(p_ref[sub, :].astype(BF16), w_proj), _dot(h_ref[sub, :], wg_ref[...])

    def finish(r0, z):
        sub = slice(r0, r0 + SUB_ROWS)
        o_ref[sub, :] = x_ref[sub, :] + _sigmoid(z[1]) * _rms(z[0], g_ref[...])

    _software_pipeline(range(0, x_ref.shape[0], SUB_ROWS), project, finish)


def _ple(x2, h3, p, w_gate, w_proj, g, tm):
    m = x2.shape[0]
    row = lambda i: (i, 0)
    return pl.pallas_call(
        _ple_kernel,
        grid=(m // tm,),
        in_specs=[
            pl.BlockSpec((tm, D_MODEL), row),
            pl.BlockSpec((tm, D_MODEL), row),
            pl.BlockSpec((tm, PLE_DIM), row),
            _resident((D_MODEL, D_MODEL)),
            _resident((PLE_DIM, D_MODEL)),
            _resident((1, D_MODEL)),
        ],
        out_specs=pl.BlockSpec((tm, D_MODEL), row),
        out_shape=jax.ShapeDtypeStruct((m, D_MODEL), F32),
        compiler_params=_params("parallel"),
        name="ple",
    )(x2, h3, p, w_gate, w_proj, g)


def _layer(x, p, norm_mix_g, w_in, gmlp_ln_g, gmlp_ln_b, gmlp_w_s, gmlp_b_s, fox_b_f, q_norm_g,
           k_norm_g, w_branch_a, w_branch_b, w_out, norm_ffn_g, w_up, conv_w, conv_b, w_down,
           ple_proj, ple_norm_g, ple_gate_norm_g, w_ple_gate, *, batch, seq):
    row_vec = lambda v: v.reshape(1, -1)
    whole = lambda w: CastJob(w, 0, w.shape[0])
    w_in_t = w_in.T
    w_uv_t, w_f_t = lax.optimization_barrier((w_in_t[:Q_COLS], w_in_t[F_COLS:G_COLS]))
    w_uv_t = w_uv_t.astype(BF16)
    w_f_t = jnp.pad(w_f_t, ((0, LANES - HEADS), (0, 0))).astype(BF16)
    b_f = jnp.pad(fox_b_f, (0, LANES - HEADS)).reshape(1, LANES)

    o_a, h, (w_qkv_t, w_down_bf) = _gmlp(
        x, row_vec(norm_mix_g), w_uv_t, row_vec(gmlp_ln_g), row_vec(gmlp_ln_b), gmlp_w_s, gmlp_b_s.T, tm=512,
        casts=[CastJob(w_in_t, Q_COLS, 3 * ATTN_WIDTH), whole(w_down)])
    q_aug, k_aug, v, (w_g_t,) = _qkv(
        h, w_qkv_t, w_f_t, b_f, row_vec(q_norm_g), row_vec(k_norm_g), tm=512, seq=seq,
        casts=[CastJob(w_in_t, G_COLS, 2 * D_MODEL)])
    o_b, (w_up_bf, w_ba_bf, w_bb_bf) = _attention(
        q_aug, k_aug, v, batch, seq, tq=512, casts=[whole(w_up), whole(w_branch_a), whole(w_branch_b)])
    y, (w_out_bf, w_gate_bf) = _merge(
        h, o_a, o_b, w_g_t, w_ba_bf, w_bb_bf, tm=1024, tn=512, casts=[whole(w_out), whole(w_ple_gate)])
    x1, h2 = _outproj(x, y, w_out_bf, row_vec(norm_ffn_g), tm=512)
    hidden = _ffn_up(h2, w_up_bf, conv_w, row_vec(conv_b), seq=seq, tf=512)
    x2, h3 = _ffn_down(x1, hidden, w_down_bf, row_vec(ple_gate_norm_g), tm=256)
    return _ple(x2, h3, p, w_gate_bf, ple_proj, row_vec(ple_norm_g), tm=512)


def kernel(x, p, norm_mix_g, w_in, gmlp_ln_g, gmlp_ln_b, gmlp_w_s, gmlp_b_s, fox_b_f, q_norm_g, k_norm_g, w_branch_a, w_branch_b, w_out, norm_ffn_g, w_up, conv_w, conv_b, w_down, ple_proj, ple_norm_g, ple_gate_norm_g, w_ple_gate):
    batch, seq, d = x.shape
    depth = w_in.shape[0]
    xs = x.reshape(batch * seq, d)
    for i in range(depth):
        xs = _layer(xs, p[i].reshape(batch * seq, -1), norm_mix_g[i], w_in[i], gmlp_ln_g[i], gmlp_ln_b[i],
                    gmlp_w_s[i], gmlp_b_s[i], fox_b_f[i], q_norm_g[i], k_norm_g[i], w_branch_a[i],
                    w_branch_b[i], w_out[i], norm_ffn_g[i], w_up[i], conv_w[i], conv_b[i], w_down[i],
                    ple_proj[i], ple_norm_g[i], ple_gate_norm_g[i], w_ple_gate[i], batch=batch, seq=seq)
    return xs.reshape(batch, seq, d)
```

```python
import functools
import math
from typing import NamedTuple

import jax
import jax.numpy as jnp
import numpy as np
from jax import lax
from jax.experimental import pallas as pl
from jax.experimental.pallas import tpu as pltpu

D_MODEL = 2048
CHUNK = 128
GROUPS = 8
GROUP_DIM = 128
GMLP_WIDTH = GROUPS * GROUP_DIM
HEADS = 8
HEAD_DIM = 128
ATTN_WIDTH = HEADS * HEAD_DIM
D_FF = 5632
PLE_DIM = 256
EPS = 1e-6
Q_COLS = 2 * GMLP_WIDTH
F_COLS = Q_COLS + 3 * ATTN_WIDTH
G_COLS = F_COLS + HEADS
LANES = 128
AUG = 2 * HEAD_DIM
N_SPLIT = 3
LOG2_E = 1.4426950408889634
DOT_ROWS = 512
SUB_ROWS = 256

VMEM_LIMIT = 56 * 1024 * 1024

F32 = jnp.float32
BF16 = jnp.bfloat16
NT = (((1,), (1,)), ((), ()))


def _params(*semantics):
    return pltpu.CompilerParams(dimension_semantics=semantics, vmem_limit_bytes=VMEM_LIMIT)


def _resident(shape, block=None):
    index = tuple(block) if block is not None else (0,) * len(shape)
    return pl.BlockSpec(shape, lambda *_: index, pipeline_mode=pl.Buffered(1))


def _dot(a, b):
    return jnp.dot(a, b, preferred_element_type=F32)


def _dot_nt(a, b):
    return lax.dot_general(a, b, NT, preferred_element_type=F32)


def _software_pipeline(items, produce, consume):
    pending = None
    for item in items:
        produced = produce(item)
        if pending is not None:
            consume(*pending)
        pending = (item, produced)
    consume(*pending)


def _rms(xf, g):
    return xf * lax.rsqrt(jnp.mean(xf * xf, axis=-1, keepdims=True) + EPS) * g


def _gelu_erf(x):
    return 0.5 * x * (1.0 + lax.erf(x * (2.0 ** -0.5)))


def _gelu_tanh(x):
    return 0.5 * x * (1.0 + jnp.tanh((2.0 / jnp.pi) ** 0.5 * (x + 0.044715 * (x * x * x))))


def _sigmoid(x):
    return 1.0 / (1.0 + jnp.exp(-x))


def _log_sigmoid(x):
    return -(jnp.maximum(-x, 0.0) + jnp.log1p(jnp.exp(-jnp.abs(x))))


class CastJob(NamedTuple):
    src: jax.Array
    row_start: int
    rows: int
    col_start: int
    cols: int


def _hosted_call(kernel_fn, *, grid, in_specs, out_specs, out_shape, semantics, name, casts=(),
                 scratch_shapes=()):
    n_in, n_out, n_cast = len(in_specs), len(out_specs), len(casts)
    n_steps = math.prod(grid)

    def step_of(*ids):
        step = ids[0]
        for extent, idx in zip(grid[1:], ids[1:]):
            step = step * extent + idx
        return step

    cast_in, cast_out, cast_shape = [], [], []
    for job in casts:
        slab, rem = divmod(job.rows, n_steps)
        assert rem == 0 and slab % 16 == 0 and job.row_start % 8 == 0 and job.col_start % LANES == 0, (name, job[1:])
        cast_in.append(pl.BlockSpec(
            (pl.Element(slab), pl.Element(job.cols)),
            lambda *ids, job=job, slab=slab: (pl.multiple_of(job.row_start + step_of(*ids) * slab, 8), job.col_start)))
        cast_out.append(pl.BlockSpec((slab, job.cols), lambda *ids: (step_of(*ids), 0)))
        cast_shape.append(jax.ShapeDtypeStruct((job.rows, job.cols), BF16))

    def body(*refs):
        ins, refs = refs[:n_in], refs[n_in:]
        srcs, refs = refs[:n_cast], refs[n_cast:]
        outs, refs = refs[:n_out], refs[n_out:]
        dsts, scratch = refs[:n_cast], refs[n_cast:]
        for src, dst in zip(srcs, dsts):
            dst[...] = src[...].astype(BF16)
        kernel_fn(*ins, *outs, *scratch)

    call = pl.pallas_call(
        body,
        grid=grid,
        in_specs=list(in_specs) + cast_in,
        out_specs=list(out_specs) + cast_out,
        out_shape=list(out_shape) + cast_shape,
        scratch_shapes=scratch_shapes,
        compiler_params=_params(*semantics),
        name=name,
    )

    def run(*operands):
        results = call(*operands, *(job.src for job in casts))
        return results[:n_out], results[n_out:]

    return run


def _gmlp_kernel(x_ref, g_ref, w_ref, lng_ref, lnb_ref, ws_ref, bst_ref, oa_ref, h_ref):
    tm = x_ref.shape[0]
    row = lax.broadcasted_iota(jnp.int32, (CHUNK, CHUNK), 0)
    col = lax.broadcasted_iota(jnp.int32, (CHUNK, CHUNK), 1)
    causal = row >= col
    w_mix = [jnp.where(causal, ws_ref[grp], 0.0).astype(BF16) for grp in range(GROUPS)]

    def project(r0):
        sub = slice(r0, r0 + SUB_ROWS)
        h = _rms(x_ref[sub, :], g_ref[...]).astype(BF16)
        h_ref[sub, :] = h
        return _dot_nt(h, w_ref[...])

    def mix(r0, z):
        u = _gelu_erf(z[:, :GMLP_WIDTH])
        v = _gelu_erf(z[:, GMLP_WIDTH:])
        mu = jnp.mean(v, axis=-1, keepdims=True)
        vc = v - mu
        vn = vc * lax.rsqrt(jnp.mean(vc * vc, axis=-1, keepdims=True) + EPS)
        vn = (vn * lng_ref[...] + lnb_ref[...]).astype(BF16)
        for grp in range(GROUPS):
            cols = slice(grp * GROUP_DIM, (grp + 1) * GROUP_DIM)
            bias = bst_ref[:, grp:grp + 1]
            for c0 in range(0, SUB_ROWS, CHUNK):
                rows = slice(c0, c0 + CHUNK)
                mixed = _dot(w_mix[grp], vn[rows, cols]) + bias
                oa_ref[r0 + c0:r0 + c0 + CHUNK, cols] = (u[rows, cols] * mixed).astype(BF16)

    _software_pipeline(range(0, tm, SUB_ROWS), project, mix)


def _gmlp(x, g, w_uv_t, ln_g, ln_b, w_s, b_s_t, tm, casts):
    m = x.shape[0]
    row = lambda i: (i, 0)
    (o_a, h), cast = _hosted_call(
        _gmlp_kernel,
        grid=(m // tm,),
        in_specs=[
            pl.BlockSpec((tm, D_MODEL), row),
            _resident((1, D_MODEL)),
            _resident((2 * GMLP_WIDTH, D_MODEL)),
            _resident((1, GMLP_WIDTH)),
            _resident((1, GMLP_WIDTH)),
            _resident((GROUPS, CHUNK, CHUNK)),
            _resident((CHUNK, GROUPS)),
        ],
        out_specs=[pl.BlockSpec((tm, GMLP_WIDTH), row), pl.BlockSpec((tm, D_MODEL), row)],
        out_shape=[jax.ShapeDtypeStruct((m, GMLP_WIDTH), BF16), jax.ShapeDtypeStruct((m, D_MODEL), BF16)],
        semantics=("parallel",),
        name="gmlp",
        casts=casts,
    )(x, g, w_uv_t, ln_g, ln_b, w_s, b_s_t)
    return o_a, h, cast


def _split_bf16(x):
    pieces = []
    for _ in range(N_SPLIT):
        piece = x.astype(BF16)
        pieces.append(piece)
        x = x - piece.astype(F32)
    return pieces


def _decay_constants(tm):
    lower = np.tril(np.ones((tm, tm), np.float32))
    sel_q = np.zeros((LANES, HEADS * HEAD_DIM), np.float32)
    sel_k = np.zeros_like(sel_q)
    ones_q = np.zeros((1, HEAD_DIM), np.float32)
    ones_k = np.zeros((1, HEAD_DIM), np.float32)
    for piece in range(N_SPLIT):
        ones_q[0, N_SPLIT + piece] = 1.0
        ones_k[0, piece] = 1.0
        for head in range(HEADS):
            sel_q[piece * HEADS + head, head * HEAD_DIM + piece] = 1.0
            sel_k[piece * HEADS + head, head * HEAD_DIM + N_SPLIT + piece] = -1.0
    as_bf16 = lambda a: jnp.asarray(a, BF16)
    return as_bf16(lower), as_bf16(sel_q), as_bf16(sel_k), jnp.asarray(ones_q), jnp.asarray(ones_k)


def _qkv_kernel(h_ref, wq_ref, wk_ref, wv_ref, wf_ref, bf_ref, qg_ref, kg_ref, lower_ref, selq_ref, selk_ref,
                oneq_ref, onek_ref, qa_ref, ka_ref, v_ref, carry_ref, *, tiles_per_seq):
    tm = h_ref.shape[0]
    h = h_ref[...]

    @pl.when(pl.program_id(0) % tiles_per_seq == 0)
    def _():
        carry_ref[...] = jnp.zeros_like(carry_ref)

    f_logit = _dot_nt(h, wf_ref[...]) + bf_ref[...]
    z_q = _dot_nt(h, wq_ref[...])
    lower = lower_ref[...]
    local = sum(_dot(lower, piece) for piece in _split_bf16(_log_sigmoid(f_logit)))
    z_k = _dot_nt(h, wk_ref[...])
    cum = local + carry_ref[...]
    carry_ref[...] = cum[tm - 1:tm, :]
    lane = lax.broadcasted_iota(jnp.int32, cum.shape, 1)
    pieces = _split_bf16(cum * (HEAD_DIM ** 0.5))
    for piece in range(1, N_SPLIT):
        pieces[0] = jnp.where(lane >= piece * HEADS, pieces[piece], pieces[0])
    pieces = pieces[0]
    decay_q = _dot(pieces, selq_ref[...])
    decay_k = _dot(pieces, selk_ref[...])
    z_v = _dot_nt(h, wv_ref[...])

    for z, decay, gain_ref, ones_ref, out_ref in ((z_q, decay_q, qg_ref, oneq_ref, qa_ref),
                                                  (z_k, decay_k, kg_ref, onek_ref, ka_ref)):
        for hd in range(HEADS):
            cols = slice(hd * HEAD_DIM, (hd + 1) * HEAD_DIM)
            out_ref[:, hd * AUG:hd * AUG + HEAD_DIM] = _rms(z[:, cols], gain_ref[...]).astype(BF16)
            out_ref[:, hd * AUG + HEAD_DIM:(hd + 1) * AUG] = (decay[:, cols] + ones_ref[...]).astype(BF16)
    v_ref[...] = z_v.astype(BF16)


def _qkv(h, w_qkv_t, w_f_t, b_f, q_g, k_g, tm, seq, casts):
    m = h.shape[0]
    row = lambda i: (i, 0)
    consts = _decay_constants(tm)
    (q_aug, k_aug, v), cast = _hosted_call(
        functools.partial(_qkv_kernel, tiles_per_seq=seq // tm),
        grid=(m // tm,),
        in_specs=[
            pl.BlockSpec((tm, D_MODEL), row),
            _resident((ATTN_WIDTH, D_MODEL), (0, 0)),
            _resident((ATTN_WIDTH, D_MODEL), (1, 0)),
            _resident((ATTN_WIDTH, D_MODEL), (2, 0)),
            _resident((LANES, D_MODEL)),
            _resident((1, LANES)),
            _resident((1, HEAD_DIM)),
            _resident((1, HEAD_DIM)),
        ] + [_resident(c.shape) for c in consts],
        out_specs=[
            pl.BlockSpec((tm, HEADS * AUG), row),
            pl.BlockSpec((tm, HEADS * AUG), row),
            pl.BlockSpec((tm, ATTN_WIDTH), row),
        ],
        out_shape=[
            jax.ShapeDtypeStruct((m, HEADS * AUG), BF16),
            jax.ShapeDtypeStruct((m, HEADS * AUG), BF16),
            jax.ShapeDtypeStruct((m, ATTN_WIDTH), BF16),
        ],
        scratch_shapes=[pltpu.VMEM((1, LANES), F32)],
        semantics=("arbitrary",),
        name="qkv",
        casts=casts,
    )(h, w_qkv_t, w_qkv_t, w_qkv_t, w_f_t, b_f, q_g, k_g, *consts)
    return q_aug, k_aug, v, cast


def _attn_kernel(q_ref, k_ref, v_ref, o_ref, *, tq):
    seq = q_ref.shape[0]
    to_log2 = (HEAD_DIM ** -0.5) * LOG2_E
    row = lax.broadcasted_iota(jnp.int32, (tq, tq), 0)
    col = lax.broadcasted_iota(jnp.int32, (tq, tq), 1)
    keep = row >= col
    v_aug = jnp.concatenate([v_ref[...], jnp.ones((seq, HEAD_DIM), BF16)], axis=1)
    for qi in reversed(range(seq // tq)):
        q0, q1 = qi * tq, (qi + 1) * tq
        q = q_ref[q0:q1, :]
        diag = jnp.where(keep, _dot_nt(q, k_ref[q0:q1, :]), -jnp.inf)
        m = jnp.max(diag, axis=-1, keepdims=True)
        if qi:
            past = _dot_nt(q, k_ref[:q0, :])
            m = jnp.maximum(m, jnp.max(past, axis=-1, keepdims=True))
        acc = _dot(jnp.exp2((diag - m) * to_log2).astype(BF16), v_aug[q0:q1, :])
        if qi:
            acc = acc + _dot(jnp.exp2((past - m) * to_log2).astype(BF16), v_aug[:q0, :])
        o_ref[q0:q1, :] = (acc[:, :HEAD_DIM] / acc[:, HEAD_DIM:]).astype(BF16)


def _attention(q_aug, k_aug, v, batch, seq, tq, casts):
    m = v.shape[0]
    head_block = lambda b, hd: (b, hd)
    (o_b,), cast = _hosted_call(
        functools.partial(_attn_kernel, tq=tq),
        grid=(batch, HEADS),
        in_specs=[
            pl.BlockSpec((seq, AUG), head_block),
            pl.BlockSpec((seq, AUG), head_block),
            pl.BlockSpec((seq, HEAD_DIM), head_block),
        ],
        out_specs=[pl.BlockSpec((seq, HEAD_DIM), head_block)],
        out_shape=[jax.ShapeDtypeStruct((m, ATTN_WIDTH), BF16)],
        semantics=("parallel", "parallel"),
        name="attention",
        casts=casts,
    )(q_aug, k_aug, v)
    return o_b, cast


def _merge_kernel(h_ref, oa_ref, ob_ref, wga_ref, wgb_ref, wba_ref, wbb_ref, y_ref):
    h = h_ref[...]
    ya = _sigmoid(_dot_nt(h, wga_ref[...])) * _dot(oa_ref[...], wba_ref[...])
    yb = _sigmoid(_dot_nt(h, wgb_ref[...])) * _dot(ob_ref[...], wbb_ref[...])
    y_ref[...] = (ya + yb).astype(BF16)


def _merge(h, o_a, o_b, w_g_t, w_ba, w_bb, tm, tn, casts):
    m = h.shape[0]
    nn = D_MODEL // tn
    row = lambda i, j: (i, 0)
    col = lambda i, j: (0, j)
    (y,), cast = _hosted_call(
        _merge_kernel,
        grid=(m // tm, nn),
        in_specs=[
            pl.BlockSpec((tm, D_MODEL), row),
            pl.BlockSpec((tm, GMLP_WIDTH), row),
            pl.BlockSpec((tm, ATTN_WIDTH), row),
            pl.BlockSpec((tn, D_MODEL), lambda i, j: (j, 0)),
            pl.BlockSpec((tn, D_MODEL), lambda i, j: (nn + j, 0)),
            pl.BlockSpec((GMLP_WIDTH, tn), col),
            pl.BlockSpec((ATTN_WIDTH, tn), col),
        ],
        out_specs=[pl.BlockSpec((tm, tn), lambda i, j: (i, j))],
        out_shape=[jax.ShapeDtypeStruct((m, D_MODEL), BF16)],
        semantics=("parallel", "arbitrary"),
        name="merge",
        casts=casts,
    )(h, o_a, o_b, w_g_t, w_g_t, w_ba, w_bb)
    return y, cast


def _outproj_kernel(x_ref, y_ref, w_ref, g_ref, x1_ref, h2_ref):
    def project(r0):
        return _dot(y_ref[r0:r0 + SUB_ROWS, :], w_ref[...])

    def finish(r0, z):
        sub = slice(r0, r0 + SUB_ROWS)
        x1 = x_ref[sub, :] + z
        x1_ref[sub, :] = x1
        h2_ref[sub, :] = _rms(x1, g_ref[...]).astype(BF16)

    _software_pipeline(range(0, x_ref.shape[0], SUB_ROWS), project, finish)


def _outproj(x, y, w_out, g_next, tm):
    m = x.shape[0]
    row = lambda i: (i, 0)
    return pl.pallas_call(
        _outproj_kernel,
        grid=(m // tm,),
        in_specs=[
            pl.BlockSpec((tm, D_MODEL), row),
            pl.BlockSpec((tm, D_MODEL), row),
            _resident((D_MODEL, D_MODEL)),
            _resident((1, D_MODEL)),
        ],
        out_specs=[pl.BlockSpec((tm, D_MODEL), row), pl.BlockSpec((tm, D_MODEL), row)],
        out_shape=[jax.ShapeDtypeStruct((m, D_MODEL), F32), jax.ShapeDtypeStruct((m, D_MODEL), BF16)],
        compiler_params=_params("parallel"),
        name="outproj",
    )(x, y, w_out, g_next)


def _ffn_up_kernel(h_ref, wa_ref, wb_ref, cw_ref, cb_ref, o_ref):
    seq = h_ref.shape[0]

    def proj(w):
        return jnp.concatenate([_dot(h_ref[r:r + DOT_ROWS, :], w) for r in range(0, seq, DOT_ROWS)], axis=0)

    a = proj(wa_ref[...])
    t = lax.broadcasted_iota(jnp.int32, a.shape, 0)
    a1 = jnp.where(t >= 1, pltpu.roll(a, 1, 0), 0.0)
    a2 = jnp.where(t >= 2, pltpu.roll(a, 2, 0), 0.0)
    conv = cb_ref[...] + a2 * cw_ref[0:1, :] + a1 * cw_ref[1:2, :] + a * cw_ref[2:3, :]
    o_ref[...] = (_gelu_tanh(conv) * proj(wb_ref[...])).astype(BF16)


def _ffn_up(h2, w_up_a, w_up_b, conv_w, conv_b, seq, tf, casts):
    m = h2.shape[0]
    col = lambda b, j: (0, j)
    (hidden,), cast = _hosted_call(
        _ffn_up_kernel,
        grid=(m // seq, D_FF // tf),
        in_specs=[
            pl.BlockSpec((seq, D_MODEL), lambda b, j: (b, 0)),
            pl.BlockSpec((D_MODEL, tf), col),
            pl.BlockSpec((D_MODEL, tf), col),
            pl.BlockSpec((conv_w.shape[0], tf), col),
            pl.BlockSpec((1, tf), col),
        ],
        out_specs=[pl.BlockSpec((seq, tf), lambda b, j: (b, j))],
        out_shape=[jax.ShapeDtypeStruct((m, D_FF), BF16)],
        semantics=("parallel", "arbitrary"),
        name="ffn_up",
        casts=casts,
    )(h2, w_up_a, w_up_b, conv_w, conv_b)
    return hidden, cast


def _ffn_down_kernel(x_ref, hid_ref, w_ref, g_ref, x2_ref, h3_ref):
    x2 = x_ref[...] + _dot(hid_ref[...], w_ref[...])
    x2_ref[...] = x2
    h3_ref[...] = _rms(x2, g_ref[...]).astype(BF16)


def _ffn_down(x1, hidden, w_down, g_next, tm, casts):
    m = x1.shape[0]
    row = lambda i: (i, 0)
    (x2, h3), cast = _hosted_call(
        _ffn_down_kernel,
        grid=(m // tm,),
        in_specs=[
            pl.BlockSpec((tm, D_MODEL), row),
            pl.BlockSpec((tm, D_FF), row),
            _resident((D_FF, D_MODEL)),
            _resident((1, D_MODEL)),
        ],
        out_specs=[pl.BlockSpec((tm, D_MODEL), row), pl.BlockSpec((tm, D_MODEL), row)],
        out_shape=[jax.ShapeDtypeStruct((m, D_MODEL), F32), jax.ShapeDtypeStruct((m, D_MODEL), BF16)],
        semantics=("parallel",),
        name="ffn_down",
        casts=casts,
    )(x1, hidden, w_down, g_next)
    return x2, h3, cast


def _ple_kernel(x_ref, h_ref, p_ref, wg_ref, wp_ref, g_ref, o_ref):
    w_proj = wp_ref[...].astype(BF16)

    def project(r0):
        sub = slice(r0, r0 + SUB_ROWS)
        return _dot(p_ref[sub, :].astype(BF16), w_proj), _dot(h_ref[sub, :], wg_ref[...])

    def finish(r0, z):
        sub = slice(r0, r0 + SUB_ROWS)
        o_ref[sub, :] = x_ref[sub, :] + _sigmoid(z[1]) * _rms(z[0], g_ref[...])

    _software_pipeline(range(0, x_ref.shape[0], SUB_ROWS), project, finish)


def _ple(x2, h3, p, w_gate, w_proj, g, tm):
    m = x2.shape[0]
    row = lambda i: (i, 0)
    return pl.pallas_call(
        _ple_kernel,
        grid=(m // tm,),
        in_specs=[
            pl.BlockSpec((tm, D_MODEL), row),
            pl.BlockSpec((tm, D_MODEL), row),
            pl.BlockSpec((tm, PLE_DIM), row),
            _resident((D_MODEL, D_MODEL)),
            _resident((PLE_DIM, D_MODEL)),
            _resident((1, D_MODEL)),
        ],
        out_specs=pl.BlockSpec((tm, D_MODEL), row),
        out_shape=jax.ShapeDtypeStruct((m, D_MODEL), F32),
        compiler_params=_params("parallel"),
        name="ple",
    )(x2, h3, p, w_gate, w_proj, g)


def _layer(x, p, norm_mix_g, w_in, gmlp_ln_g, gmlp_ln_b, gmlp_w_s, gmlp_b_s, fox_b_f, q_norm_g,
           k_norm_g, w_branch_a, w_branch_b, w_out, norm_ffn_g, w_up, conv_w, conv_b, w_down,
           ple_proj, ple_norm_g, ple_gate_norm_g, w_ple_gate, *, batch, seq):
    row_vec = lambda v: v.reshape(1, -1)
    whole = lambda w: CastJob(w, 0, w.shape[0], 0, w.shape[1])
    w_in_t = w_in.T
    w_uv_t, w_f_t = lax.optimization_barrier((w_in_t[:Q_COLS], w_in_t[F_COLS:G_COLS]))
    w_uv_t = w_uv_t.astype(BF16)
    w_f_t = jnp.pad(jnp.tile(w_f_t, (N_SPLIT, 1)), ((0, LANES - N_SPLIT * HEADS), (0, 0))).astype(BF16)
    b_f = jnp.pad(jnp.tile(fox_b_f, N_SPLIT), (0, LANES - N_SPLIT * HEADS)).reshape(1, LANES)

    o_a, h, (w_qkv_t,) = _gmlp(
        x, row_vec(norm_mix_g), w_uv_t, row_vec(gmlp_ln_g), row_vec(gmlp_ln_b), gmlp_w_s, gmlp_b_s.T, tm=512,
        casts=[CastJob(w_in_t, Q_COLS, 3 * ATTN_WIDTH, 0, D_MODEL)])
    q_aug, k_aug, v, (w_g_t, w_ba_bf, w_bb_bf) = _qkv(
        h, w_qkv_t, w_f_t, b_f, row_vec(q_norm_g), row_vec(k_norm_g), tm=512, seq=seq,
        casts=[CastJob(w_in_t, G_COLS, 2 * D_MODEL, 0, D_MODEL), whole(w_branch_a), whole(w_branch_b)])
    o_b, (w_up_a,) = _attention(
        q_aug, k_aug, v, batch, seq, tq=512, casts=[CastJob(w_up, 0, D_MODEL, 0, D_FF)])
    y, (w_out_bf, w_up_b) = _merge(
        h, o_a, o_b, w_g_t, w_ba_bf, w_bb_bf, tm=1024, tn=512,
        casts=[whole(w_out), CastJob(w_up, 0, D_MODEL, D_FF, D_FF)])
    x1, h2 = _outproj(x, y, w_out_bf, row_vec(norm_ffn_g), tm=512)
    hidden, (w_down_bf,) = _ffn_up(h2, w_up_a, w_up_b, conv_w, row_vec(conv_b), seq=seq, tf=512,
                                   casts=[whole(w_down)])
    x2, h3, (w_gate_bf,) = _ffn_down(x1, hidden, w_down_bf, row_vec(ple_gate_norm_g), tm=256,
                                     casts=[whole(w_ple_gate)])
    return _ple(x2, h3, p, w_gate_bf, ple_proj, row_vec(ple_norm_g), tm=512)


def kernel(x, p, norm_mix_g, w_in, gmlp_ln_g, gmlp_ln_b, gmlp_w_s, gmlp_b_s, fox_b_f, q_norm_g, k_norm_g, w_branch_a, w_branch_b, w_out, norm_ffn_g, w_up, conv_w, conv_b, w_down, ple_proj, ple_norm_g, ple_gate_norm_g, w_ple_gate):
    batch, seq, d = x.shape
    depth = w_in.shape[0]
    xs = x.reshape(batch * seq, d)
    for i in range(depth):
        xs = _layer(xs, p[i].reshape(batch * seq, -1), norm_mix_g[i], w_in[i], gmlp_ln_g[i], gmlp_ln_b[i],
                    gmlp_w_s[i], gmlp_b_s[i], fox_b_f[i], q_norm_g[i], k_norm_g[i], w_branch_a[i],
                    w_branch_b[i], w_out[i], norm_ffn_g[i], w_up[i], conv_w[i], conv_b[i], w_down[i],
                    ple_proj[i], ple_norm_g[i], ple_gate_norm_g[i], w_ple_gate[i], batch=batch, seq=seq)
    return xs.reshape(batch, seq, d)
```

```python
import functools
import math
from typing import NamedTuple

import jax
import jax.numpy as jnp
import numpy as np
from jax import lax
from jax.experimental import pallas as pl
from jax.experimental.pallas import tpu as pltpu

D_MODEL = 2048
CHUNK = 128
GROUPS = 8
GROUP_DIM = 128
GMLP_WIDTH = GROUPS * GROUP_DIM
HEADS = 8
HEAD_DIM = 128
ATTN_WIDTH = HEADS * HEAD_DIM
D_FF = 5632
PLE_DIM = 256
EPS = 1e-6
Q_COLS = 2 * GMLP_WIDTH
F_COLS = Q_COLS + 3 * ATTN_WIDTH
G_COLS = F_COLS + HEADS
LANES = 128
F32_SUBLANES = 8
BF16_SUBLANES = 2 * F32_SUBLANES
AUG = 2 * HEAD_DIM
N_SPLIT = 3
LOG2_E = 1.4426950408889634
DOT_ROWS = 512
SUB_ROWS = 256

VMEM_LIMIT = 56 * 1024 * 1024


class Tiles(NamedTuple):
    gmlp_rows: int = 512
    qkv_rows: int = 512
    attn_query_rows: int = 512
    merge_rows: int = 1024
    merge_cols: int = 512
    outproj_rows: int = 512
    ffn_cols: int = 512
    ffn_down_rows: int = 256
    ple_rows: int = 512


TILES = Tiles()

F32 = jnp.float32
BF16 = jnp.bfloat16
NT = (((1,), (1,)), ((), ()))


def _params(*semantics):
    return pltpu.CompilerParams(dimension_semantics=semantics, vmem_limit_bytes=VMEM_LIMIT)


def _resident(shape, block=None):
    index = tuple(block) if block is not None else (0,) * len(shape)
    return pl.BlockSpec(shape, lambda *_: index, pipeline_mode=pl.Buffered(1))


def _dot(a, b):
    return jnp.dot(a, b, preferred_element_type=F32)


def _dot_nt(a, b):
    return lax.dot_general(a, b, NT, preferred_element_type=F32)


def _software_pipeline(items, produce, consume):
    pending = None
    for item in items:
        produced = produce(item)
        if pending is not None:
            consume(*pending)
        pending = (item, produced)
    consume(*pending)


def _rms(xf, g):
    return xf * lax.rsqrt(jnp.mean(xf * xf, axis=-1, keepdims=True) + EPS) * g


def _gelu_erf(x):
    return 0.5 * x * (1.0 + lax.erf(x * (2.0 ** -0.5)))


def _gelu_tanh(x):
    return 0.5 * x * (1.0 + jnp.tanh((2.0 / jnp.pi) ** 0.5 * (x + 0.044715 * (x * x * x))))


def _sigmoid(x):
    return 1.0 / (1.0 + jnp.exp(-x))


def _log_sigmoid(x):
    return -(jnp.maximum(-x, 0.0) + jnp.log1p(jnp.exp(-jnp.abs(x))))


class CastJob(NamedTuple):
    src: jax.Array
    row_start: int
    rows: int
    col_start: int
    cols: int


def _hosted_call(kernel_fn, *, grid, in_specs, out_specs, out_shape, semantics, name, casts=(),
                 scratch_shapes=()):
    n_in, n_out, n_cast = len(in_specs), len(out_specs), len(casts)
    n_steps = math.prod(grid)

    def step_of(*ids):
        step = ids[0]
        for extent, idx in zip(grid[1:], ids[1:]):
            step = step * extent + idx
        return step

    cast_in, cast_out, cast_shape = [], [], []
    for job in casts:
        slab, rem = divmod(job.rows, n_steps)
        assert rem == 0 and slab % BF16_SUBLANES == 0, (name, job[1:], n_steps)
        assert job.row_start % F32_SUBLANES == 0 and job.col_start % LANES == 0, (name, job[1:])
        cast_in.append(pl.BlockSpec(
            (pl.Element(slab), pl.Element(job.cols)),
            lambda *ids, job=job, slab=slab: (
                pl.multiple_of(job.row_start + step_of(*ids) * slab, F32_SUBLANES), job.col_start)))
        cast_out.append(pl.BlockSpec((slab, job.cols), lambda *ids: (step_of(*ids), 0)))
        cast_shape.append(jax.ShapeDtypeStruct((job.rows, job.cols), BF16))

    def body(*refs):
        ins, refs = refs[:n_in], refs[n_in:]
        srcs, refs = refs[:n_cast], refs[n_cast:]
        outs, refs = refs[:n_out], refs[n_out:]
        dsts, scratch = refs[:n_cast], refs[n_cast:]
        kernel_fn(*ins, *outs, *scratch)
        for src, dst in zip(srcs, dsts):
            dst[...] = src[...].astype(BF16)

    call = pl.pallas_call(
        body,
        grid=grid,
        in_specs=list(in_specs) + cast_in,
        out_specs=list(out_specs) + cast_out,
        out_shape=list(out_shape) + cast_shape,
        scratch_shapes=scratch_shapes,
        compiler_params=_params(*semantics),
        name=name,
    )

    def run(*operands):
        results = call(*operands, *(job.src for job in casts))
        return results[:n_out], results[n_out:]

    return run


def _gmlp_kernel(x_ref, g_ref, w_ref, lng_ref, lnb_ref, ws_ref, bst_ref, oa_ref, h_ref):
    tm = x_ref.shape[0]
    row = lax.broadcasted_iota(jnp.int32, (CHUNK, CHUNK), 0)
    col = lax.broadcasted_iota(jnp.int32, (CHUNK, CHUNK), 1)
    causal = row >= col
    w_mix = [jnp.where(causal, ws_ref[grp], 0.0).astype(BF16) for grp in range(GROUPS)]

    def project(r0):
        sub = slice(r0, r0 + SUB_ROWS)
        h = _rms(x_ref[sub, :], g_ref[...]).astype(BF16)
        h_ref[sub, :] = h
        return _dot_nt(h, w_ref[...])

    def mix(r0, z):
        u = _gelu_erf(z[:, :GMLP_WIDTH])
        v = _gelu_erf(z[:, GMLP_WIDTH:])
        mu = jnp.mean(v, axis=-1, keepdims=True)
        vc = v - mu
        vn = vc * lax.rsqrt(jnp.mean(vc * vc, axis=-1, keepdims=True) + EPS)
        vn = (vn * lng_ref[...] + lnb_ref[...]).astype(BF16)
        for grp in range(GROUPS):
            cols = slice(grp * GROUP_DIM, (grp + 1) * GROUP_DIM)
            bias = bst_ref[:, grp:grp + 1]
            for c0 in range(0, SUB_ROWS, CHUNK):
                rows = slice(c0, c0 + CHUNK)
                mixed = _dot(w_mix[grp], vn[rows, cols]) + bias
                oa_ref[r0 + c0:r0 + c0 + CHUNK, cols] = (u[rows, cols] * mixed).astype(BF16)

    _software_pipeline(range(0, tm, SUB_ROWS), project, mix)


def _gmlp(x, g, w_uv_t, ln_g, ln_b, w_s, b_s_t, tm, casts):
    m = x.shape[0]
    row = lambda i: (i, 0)
    (o_a, h), cast = _hosted_call(
        _gmlp_kernel,
        grid=(m // tm,),
        in_specs=[
            pl.BlockSpec((tm, D_MODEL), row),
            _resident((1, D_MODEL)),
            _resident((2 * GMLP_WIDTH, D_MODEL)),
            _resident((1, GMLP_WIDTH)),
            _resident((1, GMLP_WIDTH)),
            _resident((GROUPS, CHUNK, CHUNK)),
            _resident((CHUNK, GROUPS)),
        ],
        out_specs=[pl.BlockSpec((tm, GMLP_WIDTH), row), pl.BlockSpec((tm, D_MODEL), row)],
        out_shape=[jax.ShapeDtypeStruct((m, GMLP_WIDTH), BF16), jax.ShapeDtypeStruct((m, D_MODEL), BF16)],
        semantics=("parallel",),
        name="gmlp",
        casts=casts,
    )(x, g, w_uv_t, ln_g, ln_b, w_s, b_s_t)
    return o_a, h, cast


def _split_bf16(x):
    pieces = []
    for _ in range(N_SPLIT):
        piece = x.astype(BF16)
        pieces.append(piece)
        x = x - piece.astype(F32)
    return pieces


def _decay_constants(tm):
    lower = np.tril(np.ones((tm, tm), np.float32))
    sel_q = np.zeros((LANES, HEADS * HEAD_DIM), np.float32)
    sel_k = np.zeros_like(sel_q)
    ones_q = np.zeros((1, HEAD_DIM), np.float32)
    ones_k = np.zeros((1, HEAD_DIM), np.float32)
    for piece in range(N_SPLIT):
        ones_q[0, N_SPLIT + piece] = 1.0
        ones_k[0, piece] = 1.0
        for head in range(HEADS):
            sel_q[piece * HEADS + head, head * HEAD_DIM + piece] = 1.0
            sel_k[piece * HEADS + head, head * HEAD_DIM + N_SPLIT + piece] = -1.0
    as_bf16 = lambda a: jnp.asarray(a, BF16)
    return as_bf16(lower), as_bf16(sel_q), as_bf16(sel_k), jnp.asarray(ones_q), jnp.asarray(ones_k)


def _qkv_kernel(h_ref, wq_ref, wk_ref, wv_ref, wf_ref, bf_ref, qg_ref, kg_ref, lower_ref, selq_ref, selk_ref,
                oneq_ref, onek_ref, qa_ref, ka_ref, v_ref, carry_ref, *, tiles_per_seq):
    tm = h_ref.shape[0]

    @pl.when(pl.program_id(0) % tiles_per_seq == 0)
    def _():
        carry_ref[...] = jnp.zeros_like(carry_ref)

    h = h_ref[...]

    f_logit = _dot_nt(h, wf_ref[...]) + bf_ref[...]
    z_q = _dot_nt(h, wq_ref[...])
    lower = lower_ref[...]
    local = sum(_dot(lower, piece) for piece in _split_bf16(_log_sigmoid(f_logit)))
    z_k = _dot_nt(h, wk_ref[...])
    cum = local + carry_ref[...]
    carry_ref[...] = cum[tm - 1:tm, :]
    lane = lax.broadcasted_iota(jnp.int32, cum.shape, 1)
    pieces = _split_bf16(cum * (HEAD_DIM ** 0.5))
    for piece in range(1, N_SPLIT):
        pieces[0] = jnp.where(lane >= piece * HEADS, pieces[piece], pieces[0])
    pieces = pieces[0]
    decay_q = _dot(pieces, selq_ref[...])
    decay_k = _dot(pieces, selk_ref[...])
    z_v = _dot_nt(h, wv_ref[...])

    for z, decay, gain_ref, ones_ref, out_ref in ((z_q, decay_q, qg_ref, oneq_ref, qa_ref),
                                                  (z_k, decay_k, kg_ref, onek_ref, ka_ref)):
        for hd in range(HEADS):
            cols = slice(hd * HEAD_DIM, (hd + 1) * HEAD_DIM)
            out_ref[:, hd * AUG:hd * AUG + HEAD_DIM] = _rms(z[:, cols], gain_ref[...]).astype(BF16)
            out_ref[:, hd * AUG + HEAD_DIM:(hd + 1) * AUG] = (decay[:, cols] + ones_ref[...]).astype(BF16)
    v_ref[...] = z_v.astype(BF16)


def _qkv(h, w_qkv_t, w_f_t, b_f, q_g, k_g, tm, seq, casts):
    m = h.shape[0]
    row = lambda i: (i, 0)
    consts = _decay_constants(tm)
    (q_aug, k_aug, v), cast = _hosted_call(
        functools.partial(_qkv_kernel, tiles_per_seq=seq // tm),
        grid=(m // tm,),
        in_specs=[
            pl.BlockSpec((tm, D_MODEL), row),
            _resident((ATTN_WIDTH, D_MODEL), (0, 0)),
            _resident((ATTN_WIDTH, D_MODEL), (1, 0)),
            _resident((ATTN_WIDTH, D_MODEL), (2, 0)),
            _resident((LANES, D_MODEL)),
            _resident((1, LANES)),
            _resident((1, HEAD_DIM)),
            _resident((1, HEAD_DIM)),
        ] + [_resident(c.shape) for c in consts],
        out_specs=[
            pl.BlockSpec((tm, HEADS * AUG), row),
            pl.BlockSpec((tm, HEADS * AUG), row),
            pl.BlockSpec((tm, ATTN_WIDTH), row),
        ],
        out_shape=[
            jax.ShapeDtypeStruct((m, HEADS * AUG), BF16),
            jax.ShapeDtypeStruct((m, HEADS * AUG), BF16),
            jax.ShapeDtypeStruct((m, ATTN_WIDTH), BF16),
        ],
        scratch_shapes=[pltpu.VMEM((1, LANES), F32)],
        semantics=("arbitrary",),
        name="qkv",
        casts=casts,
    )(h, w_qkv_t, w_qkv_t, w_qkv_t, w_f_t, b_f, q_g, k_g, *consts)
    return q_aug, k_aug, v, cast


def _attn_kernel(q_ref, k_ref, v_ref, o_ref, *, tq):
    seq = q_ref.shape[0]
    to_log2 = (HEAD_DIM ** -0.5) * LOG2_E
    row = lax.broadcasted_iota(jnp.int32, (tq, tq), 0)
    col = lax.broadcasted_iota(jnp.int32, (tq, tq), 1)
    keep = row >= col
    v_aug = jnp.concatenate([v_ref[...], jnp.ones((seq, HEAD_DIM), BF16)], axis=1)
    for qi in reversed(range(seq // tq)):
        q0, q1 = qi * tq, (qi + 1) * tq
        q = q_ref[q0:q1, :]
        diag = jnp.where(keep, _dot_nt(q, k_ref[q0:q1, :]), -jnp.inf)
        m = jnp.max(diag, axis=-1, keepdims=True)
        if qi:
            past = _dot_nt(q, k_ref[:q0, :])
            m = jnp.maximum(m, jnp.max(past, axis=-1, keepdims=True))
        acc = _dot(jnp.exp2((diag - m) * to_log2).astype(BF16), v_aug[q0:q1, :])
        if qi:
            acc = acc + _dot(jnp.exp2((past - m) * to_log2).astype(BF16), v_aug[:q0, :])
        o_ref[q0:q1, :] = (acc[:, :HEAD_DIM] / acc[:, HEAD_DIM:]).astype(BF16)


def _attention(q_aug, k_aug, v, batch, seq, tq, casts):
    m = v.shape[0]
    head_block = lambda b, hd: (b, hd)
    (o_b,), cast = _hosted_call(
        functools.partial(_attn_kernel, tq=tq),
        grid=(batch, HEADS),
        in_specs=[
            pl.BlockSpec((seq, AUG), head_block),
            pl.BlockSpec((seq, AUG), head_block),
            pl.BlockSpec((seq, HEAD_DIM), head_block),
        ],
        out_specs=[pl.BlockSpec((seq, HEAD_DIM), head_block)],
        out_shape=[jax.ShapeDtypeStruct((m, ATTN_WIDTH), BF16)],
        semantics=("parallel", "parallel"),
        name="attention",
        casts=casts,
    )(q_aug, k_aug, v)
    return o_b, cast


def _merge_kernel(h_ref, oa_ref, ob_ref, wga_ref, wgb_ref, wba_ref, wbb_ref, y_ref):
    h = h_ref[...]
    ya = _sigmoid(_dot_nt(h, wga_ref[...])) * _dot(oa_ref[...], wba_ref[...])
    yb = _sigmoid(_dot_nt(h, wgb_ref[...])) * _dot(ob_ref[...], wbb_ref[...])
    y_ref[...] = (ya + yb).astype(BF16)


def _merge(h, o_a, o_b, w_g_t, w_ba, w_bb, tm, tn, casts):
    m = h.shape[0]
    nn = D_MODEL // tn
    row = lambda i, j: (i, 0)
    col = lambda i, j: (0, j)
    (y,), cast = _hosted_call(
        _merge_kernel,
        grid=(m // tm, nn),
        in_specs=[
            pl.BlockSpec((tm, D_MODEL), row),
            pl.BlockSpec((tm, GMLP_WIDTH), row),
            pl.BlockSpec((tm, ATTN_WIDTH), row),
            pl.BlockSpec((tn, D_MODEL), lambda i, j: (j, 0)),
            pl.BlockSpec((tn, D_MODEL), lambda i, j: (nn + j, 0)),
            pl.BlockSpec((GMLP_WIDTH, tn), col),
            pl.BlockSpec((ATTN_WIDTH, tn), col),
        ],
        out_specs=[pl.BlockSpec((tm, tn), lambda i, j: (i, j))],
        out_shape=[jax.ShapeDtypeStruct((m, D_MODEL), BF16)],
        semantics=("parallel", "arbitrary"),
        name="merge",
        casts=casts,
    )(h, o_a, o_b, w_g_t, w_g_t, w_ba, w_bb)
    return y, cast


def _outproj_kernel(x_ref, y_ref, w_ref, g_ref, x1_ref, h2_ref):
    def project(r0):
        return _dot(y_ref[r0:r0 + SUB_ROWS, :], w_ref[...])

    def finish(r0, z):
        sub = slice(r0, r0 + SUB_ROWS)
        x1 = x_ref[sub, :] + z
        x1_ref[sub, :] = x1
        h2_ref[sub, :] = _rms(x1, g_ref[...]).astype(BF16)

    _software_pipeline(range(0, x_ref.shape[0], SUB_ROWS), project, finish)


def _outproj(x, y, w_out, g_next, tm):
    m = x.shape[0]
    row = lambda i: (i, 0)
    return pl.pallas_call(
        _outproj_kernel,
        grid=(m // tm,),
        in_specs=[
            pl.BlockSpec((tm, D_MODEL), row),
            pl.BlockSpec((tm, D_MODEL), row),
            _resident((D_MODEL, D_MODEL)),
            _resident((1, D_MODEL)),
        ],
        out_specs=[pl.BlockSpec((tm, D_MODEL), row), pl.BlockSpec((tm, D_MODEL), row)],
        out_shape=[jax.ShapeDtypeStruct((m, D_MODEL), F32), jax.ShapeDtypeStruct((m, D_MODEL), BF16)],
        compiler_params=_params("parallel"),
        name="outproj",
    )(x, y, w_out, g_next)


def _ffn_up_kernel(h_ref, wa_ref, wb_ref, cw_ref, cb_ref, o_ref):
    seq = h_ref.shape[0]

    def proj(w):
        return jnp.concatenate([_dot(h_ref[r:r + DOT_ROWS, :], w) for r in range(0, seq, DOT_ROWS)], axis=0)

    a = proj(wa_ref[...])
    t = lax.broadcasted_iota(jnp.int32, a.shape, 0)
    a1 = jnp.where(t >= 1, pltpu.roll(a, 1, 0), 0.0)
    a2 = jnp.where(t >= 2, pltpu.roll(a, 2, 0), 0.0)
    conv = cb_ref[...] + a2 * cw_ref[0:1, :] + a1 * cw_ref[1:2, :] + a * cw_ref[2:3, :]
    o_ref[...] = (_gelu_tanh(conv) * proj(wb_ref[...])).astype(BF16)


def _ffn_up(h2, w_up_a, w_up_b, conv_w, conv_b, seq, tf, casts):
    m = h2.shape[0]
    col = lambda b, j: (0, j)
    (hidden,), cast = _hosted_call(
        _ffn_up_kernel,
        grid=(m // seq, D_FF // tf),
        in_specs=[
            pl.BlockSpec((seq, D_MODEL), lambda b, j: (b, 0)),
            pl.BlockSpec((D_MODEL, tf), col),
            pl.BlockSpec((D_MODEL, tf), col),
            pl.BlockSpec((conv_w.shape[0], tf), col),
            pl.BlockSpec((1, tf), col),
        ],
        out_specs=[pl.BlockSpec((seq, tf), lambda b, j: (b, j))],
        out_shape=[jax.ShapeDtypeStruct((m, D_FF), BF16)],
        semantics=("parallel", "arbitrary"),
        name="ffn_up",
        casts=casts,
    )(h2, w_up_a, w_up_b, conv_w, conv_b)
    return hidden, cast


def _ffn_down_kernel(x_ref, hid_ref, w_ref, g_ref, x2_ref, h3_ref):
    x2 = x_ref[...] + _dot(hid_ref[...], w_ref[...])
    x2_ref[...] = x2
    h3_ref[...] = _rms(x2, g_ref[...]).astype(BF16)


def _ffn_down(x1, hidden, w_down, g_next, tm, casts):
    m = x1.shape[0]
    row = lambda i: (i, 0)
    (x2, h3), cast = _hosted_call(
        _ffn_down_kernel,
        grid=(m // tm,),
        in_specs=[
            pl.BlockSpec((tm, D_MODEL), row),
            pl.BlockSpec((tm, D_FF), row),
            _resident((D_FF, D_MODEL)),
            _resident((1, D_MODEL)),
        ],
        out_specs=[pl.BlockSpec((tm, D_MODEL), row), pl.BlockSpec((tm, D_MODEL), row)],
        out_shape=[jax.ShapeDtypeStruct((m, D_MODEL), F32), jax.ShapeDtypeStruct((m, D_MODEL), BF16)],
        semantics=("parallel",),
        name="ffn_down",
        casts=casts,
    )(x1, hidden, w_down, g_next)
    return x2, h3, cast


def _ple_kernel(x_ref, h_ref, p_ref, wg_ref, wp_ref, g_ref, o_ref):
    w_proj = wp_ref[...].astype(BF16)

    def project(r0):
        sub = slice(r0, r0 + SUB_ROWS)
        return _dot(p_ref[sub, :].astype(BF16), w_proj), _dot(h_ref[sub, :], wg_ref[...])

    def finish(r0, z):
        sub = slice(r0, r0 + SUB_ROWS)
        o_ref[sub, :] = x_ref[sub, :] + _sigmoid(z[1]) * _rms(z[0], g_ref[...])

    _software_pipeline(range(0, x_ref.shape[0], SUB_ROWS), project, finish)


def _ple(x2, h3, p, w_gate, w_proj, g, tm):
    m = x2.shape[0]
    row = lambda i: (i, 0)
    return pl.pallas_call(
        _ple_kernel,
        grid=(m // tm,),
        in_specs=[
            pl.BlockSpec((tm, D_MODEL), row),
            pl.BlockSpec((tm, D_MODEL), row),
            pl.BlockSpec((tm, PLE_DIM), row),
            _resident((D_MODEL, D_MODEL)),
            _resident((PLE_DIM, D_MODEL)),
            _resident((1, D_MODEL)),
        ],
        out_specs=pl.BlockSpec((tm, D_MODEL), row),
        out_shape=jax.ShapeDtypeStruct((m, D_MODEL), F32),
        compiler_params=_params("parallel"),
        name="ple",
    )(x2, h3, p, w_gate, w_proj, g)


def _layer(x, p, norm_mix_g, w_in, gmlp_ln_g, gmlp_ln_b, gmlp_w_s, gmlp_b_s, fox_b_f, q_norm_g,
           k_norm_g, w_branch_a, w_branch_b, w_out, norm_ffn_g, w_up, conv_w, conv_b, w_down,
           ple_proj, ple_norm_g, ple_gate_norm_g, w_ple_gate, *, batch, seq):
    row_vec = lambda v: v.reshape(1, -1)
    whole = lambda w: CastJob(w, 0, w.shape[0], 0, w.shape[1])
    w_in_t = w_in.T
    w_uv_t, w_f_t = lax.optimization_barrier((w_in_t[:Q_COLS], w_in_t[F_COLS:G_COLS]))
    w_uv_t = w_uv_t.astype(BF16)
    w_f_t = jnp.pad(jnp.tile(w_f_t, (N_SPLIT, 1)), ((0, LANES - N_SPLIT * HEADS), (0, 0))).astype(BF16)
    b_f = jnp.pad(jnp.tile(fox_b_f, N_SPLIT), (0, LANES - N_SPLIT * HEADS)).reshape(1, LANES)

    o_a, h, (w_qkv_t,) = _gmlp(
        x, row_vec(norm_mix_g), w_uv_t, row_vec(gmlp_ln_g), row_vec(gmlp_ln_b), gmlp_w_s, gmlp_b_s.T,
        tm=TILES.gmlp_rows, casts=[CastJob(w_in_t, Q_COLS, 3 * ATTN_WIDTH, 0, D_MODEL)])
    q_aug, k_aug, v, (w_g_t, w_ba_bf, w_bb_bf) = _qkv(
        h, w_qkv_t, w_f_t, b_f, row_vec(q_norm_g), row_vec(k_norm_g), tm=TILES.qkv_rows, seq=seq,
        casts=[CastJob(w_in_t, G_COLS, 2 * D_MODEL, 0, D_MODEL), whole(w_branch_a), whole(w_branch_b)])
    o_b, (w_up_a,) = _attention(
        q_aug, k_aug, v, batch, seq, tq=TILES.attn_query_rows, casts=[CastJob(w_up, 0, D_MODEL, 0, D_FF)])
    y, (w_out_bf, w_up_b) = _merge(
        h, o_a, o_b, w_g_t, w_ba_bf, w_bb_bf, tm=TILES.merge_rows, tn=TILES.merge_cols,
        casts=[whole(w_out), CastJob(w_up, 0, D_MODEL, D_FF, D_FF)])
    x1, h2 = _outproj(x, y, w_out_bf, row_vec(norm_ffn_g), tm=TILES.outproj_rows)
    hidden, (w_down_bf,) = _ffn_up(h2, w_up_a, w_up_b, conv_w, row_vec(conv_b), seq=seq, tf=TILES.ffn_cols,
                                   casts=[whole(w_down)])
    x2, h3, (w_gate_bf,) = _ffn_down(x1, hidden, w_down_bf, row_vec(ple_gate_norm_g), tm=TILES.ffn_down_rows,
                                     casts=[whole(w_ple_gate)])
    return _ple(x2, h3, p, w_gate_bf, ple_proj, row_vec(ple_norm_g), tm=TILES.ple_rows)


def kernel(x, p, norm_mix_g, w_in, gmlp_ln_g, gmlp_ln_b, gmlp_w_s, gmlp_b_s, fox_b_f, q_norm_g, k_norm_g, w_branch_a, w_branch_b, w_out, norm_ffn_g, w_up, conv_w, conv_b, w_down, ple_proj, ple_norm_g, ple_gate_norm_g, w_ple_gate):
    batch, seq, d = x.shape
    depth = w_in.shape[0]
    xs = x.reshape(batch * seq, d)
    for i in range(depth):
        xs = _layer(xs, p[i].reshape(batch * seq, -1), norm_mix_g[i], w_in[i], gmlp_ln_g[i], gmlp_ln_b[i],
                    gmlp_w_s[i], gmlp_b_s[i], fox_b_f[i], q_norm_g[i], k_norm_g[i], w_branch_a[i],
                    w_branch_b[i], w_out[i], norm_ffn_g[i], w_up[i], conv_w[i], conv_b[i], w_down[i],
                    ple_proj[i], ple_norm_g[i], ple_gate_norm_g[i], w_ple_gate[i], batch=batch, seq=seq)
    return xs.reshape(batch, seq, d)
```

```python
import functools
import math
from typing import NamedTuple

import jax
import jax.numpy as jnp
import numpy as np
from jax import lax
from jax.experimental import pallas as pl
from jax.experimental.pallas import tpu as pltpu

D_MODEL = 2048
CHUNK = 128
GROUPS = 8
GROUP_DIM = 128
GMLP_WIDTH = GROUPS * GROUP_DIM
HEADS = 8
HEAD_DIM = 128
ATTN_WIDTH = HEADS * HEAD_DIM
D_FF = 5632
PLE_DIM = 256
EPS = 1e-6
Q_COLS = 2 * GMLP_WIDTH
F_COLS = Q_COLS + 3 * ATTN_WIDTH
G_COLS = F_COLS + HEADS
LANES = 128
F32_SUBLANES = 8
BF16_SUBLANES = 2 * F32_SUBLANES
AUG = 2 * HEAD_DIM
N_SPLIT = 3
LOG2_E = 1.4426950408889634
DOT_ROWS = 512
SUB_ROWS = 256
RING_SLOTS = 3

VMEM_LIMIT = 56 * 1024 * 1024


class Tiles(NamedTuple):
    gmlp_rows: int = 512
    qkv_rows: int = 512
    attn_query_rows: int = 512
    merge_rows: int = 1024
    merge_cols: int = 512
    outproj_rows: int = 512
    ffn_cols: int = 512
    ffn_down_rows: int = 256
    ple_rows: int = 512


TILES = Tiles()

F32 = jnp.float32
BF16 = jnp.bfloat16
NT = (((1,), (1,)), ((), ()))


def _params(*semantics):
    return pltpu.CompilerParams(dimension_semantics=semantics, vmem_limit_bytes=VMEM_LIMIT)


def _resident(shape, block=None):
    index = tuple(block) if block is not None else (0,) * len(shape)
    return pl.BlockSpec(shape, lambda *_: index, pipeline_mode=pl.Buffered(1))


def _dot(a, b):
    return jnp.dot(a, b, preferred_element_type=F32)


def _dot_nt(a, b):
    return lax.dot_general(a, b, NT, preferred_element_type=F32)


def _software_pipeline(items, produce, consume):
    pending = None
    for item in items:
        produced = produce(item)
        if pending is not None:
            consume(*pending)
        pending = (item, produced)
    consume(*pending)


def _ring_fetch(src_hbm, ring_ref, sem_ref, rows, n_steps):
    step = pl.program_id(0)

    def copy(tile):
        slot = tile % RING_SLOTS
        rows_of_tile = pl.ds(pl.multiple_of(tile * rows, rows), rows)
        return pltpu.make_async_copy(src_hbm.at[rows_of_tile, :], ring_ref.at[slot], sem_ref.at[slot])

    @pl.when(step == 0)
    def _():
        for tile in range(min(RING_SLOTS - 1, n_steps)):
            copy(tile).start()

    @pl.when(step + (RING_SLOTS - 1) < n_steps)
    def _():
        copy(step + (RING_SLOTS - 1)).start()

    copy(step).wait()
    return ring_ref.at[step % RING_SLOTS]


def _rms(xf, g):
    return xf * lax.rsqrt(jnp.mean(xf * xf, axis=-1, keepdims=True) + EPS) * g


def _gelu_erf(x):
    return 0.5 * x * (1.0 + lax.erf(x * (2.0 ** -0.5)))


def _gelu_tanh(x):
    return 0.5 * x * (1.0 + jnp.tanh((2.0 / jnp.pi) ** 0.5 * (x + 0.044715 * (x * x * x))))


def _sigmoid(x):
    return 1.0 / (1.0 + jnp.exp(-x))


def _log_sigmoid(x):
    return -(jnp.maximum(-x, 0.0) + jnp.log1p(jnp.exp(-jnp.abs(x))))


class CastJob(NamedTuple):
    src: jax.Array
    row_start: int
    rows: int
    col_start: int
    cols: int


def _hosted_call(kernel_fn, *, grid, in_specs, out_specs, out_shape, semantics, name, casts=(),
                 scratch_shapes=()):
    n_in, n_out, n_cast = len(in_specs), len(out_specs), len(casts)
    n_steps = math.prod(grid)

    def step_of(*ids):
        step = ids[0]
        for extent, idx in zip(grid[1:], ids[1:]):
            step = step * extent + idx
        return step

    cast_in, cast_out, cast_shape = [], [], []
    for job in casts:
        slab, rem = divmod(job.rows, n_steps)
        assert rem == 0 and slab % BF16_SUBLANES == 0, (name, job[1:], n_steps)
        assert job.row_start % F32_SUBLANES == 0 and job.col_start % LANES == 0, (name, job[1:])
        cast_in.append(pl.BlockSpec(
            (pl.Element(slab), pl.Element(job.cols)),
            lambda *ids, job=job, slab=slab: (
                pl.multiple_of(job.row_start + step_of(*ids) * slab, F32_SUBLANES), job.col_start)))
        cast_out.append(pl.BlockSpec((slab, job.cols), lambda *ids: (step_of(*ids), 0)))
        cast_shape.append(jax.ShapeDtypeStruct((job.rows, job.cols), BF16))

    def body(*refs):
        ins, refs = refs[:n_in], refs[n_in:]
        srcs, refs = refs[:n_cast], refs[n_cast:]
        outs, refs = refs[:n_out], refs[n_out:]
        dsts, scratch = refs[:n_cast], refs[n_cast:]
        kernel_fn(*ins, *outs, *scratch)
        for src, dst in zip(srcs, dsts):
            dst[...] = src[...].astype(BF16)

    call = pl.pallas_call(
        body,
        grid=grid,
        in_specs=list(in_specs) + cast_in,
        out_specs=list(out_specs) + cast_out,
        out_shape=list(out_shape) + cast_shape,
        scratch_shapes=scratch_shapes,
        compiler_params=_params(*semantics),
        name=name,
    )

    def run(*operands):
        results = call(*operands, *(job.src for job in casts))
        return results[:n_out], results[n_out:]

    return run


def _gmlp_kernel(x_ref, g_ref, w_ref, lng_ref, lnb_ref, ws_ref, bst_ref, oa_ref, h_ref):
    tm = x_ref.shape[0]
    row = lax.broadcasted_iota(jnp.int32, (CHUNK, CHUNK), 0)
    col = lax.broadcasted_iota(jnp.int32, (CHUNK, CHUNK), 1)
    causal = row >= col
    w_mix = [jnp.where(causal, ws_ref[grp], 0.0).astype(BF16) for grp in range(GROUPS)]

    def project(r0):
        sub = slice(r0, r0 + SUB_ROWS)
        h = _rms(x_ref[sub, :], g_ref[...]).astype(BF16)
        h_ref[sub, :] = h
        return _dot_nt(h, w_ref[...])

    def mix(r0, z):
        u = _gelu_erf(z[:, :GMLP_WIDTH])
        v = _gelu_erf(z[:, GMLP_WIDTH:])
        mu = jnp.mean(v, axis=-1, keepdims=True)
        vc = v - mu
        vn = vc * lax.rsqrt(jnp.mean(vc * vc, axis=-1, keepdims=True) + EPS)
        vn = (vn * lng_ref[...] + lnb_ref[...]).astype(BF16)
        for grp in range(GROUPS):
            cols = slice(grp * GROUP_DIM, (grp + 1) * GROUP_DIM)
            bias = bst_ref[:, grp:grp + 1]
            for c0 in range(0, SUB_ROWS, CHUNK):
                rows = slice(c0, c0 + CHUNK)
                mixed = _dot(w_mix[grp], vn[rows, cols]) + bias
                oa_ref[r0 + c0:r0 + c0 + CHUNK, cols] = (u[rows, cols] * mixed).astype(BF16)

    _software_pipeline(range(0, tm, SUB_ROWS), project, mix)


def _gmlp(x, g, w_uv_t, ln_g, ln_b, w_s, b_s_t, tm, casts):
    m = x.shape[0]
    row = lambda i: (i, 0)
    (o_a, h), cast = _hosted_call(
        _gmlp_kernel,
        grid=(m // tm,),
        in_specs=[
            pl.BlockSpec((tm, D_MODEL), row),
            _resident((1, D_MODEL)),
            _resident((2 * GMLP_WIDTH, D_MODEL)),
            _resident((1, GMLP_WIDTH)),
            _resident((1, GMLP_WIDTH)),
            _resident((GROUPS, CHUNK, CHUNK)),
            _resident((CHUNK, GROUPS)),
        ],
        out_specs=[pl.BlockSpec((tm, GMLP_WIDTH), row), pl.BlockSpec((tm, D_MODEL), row)],
        out_shape=[jax.ShapeDtypeStruct((m, GMLP_WIDTH), BF16), jax.ShapeDtypeStruct((m, D_MODEL), BF16)],
        semantics=("parallel",),
        name="gmlp",
        casts=casts,
    )(x, g, w_uv_t, ln_g, ln_b, w_s, b_s_t)
    return o_a, h, cast


def _split_bf16(x):
    pieces = []
    for _ in range(N_SPLIT):
        piece = x.astype(BF16)
        pieces.append(piece)
        x = x - piece.astype(F32)
    return pieces


def _decay_constants(tm):
    lower = np.tril(np.ones((tm, tm), np.float32))
    sel_q = np.zeros((LANES, HEADS * HEAD_DIM), np.float32)
    sel_k = np.zeros_like(sel_q)
    ones_q = np.zeros((1, HEAD_DIM), np.float32)
    ones_k = np.zeros((1, HEAD_DIM), np.float32)
    for piece in range(N_SPLIT):
        ones_q[0, N_SPLIT + piece] = 1.0
        ones_k[0, piece] = 1.0
        for head in range(HEADS):
            sel_q[piece * HEADS + head, head * HEAD_DIM + piece] = 1.0
            sel_k[piece * HEADS + head, head * HEAD_DIM + N_SPLIT + piece] = -1.0
    as_bf16 = lambda a: jnp.asarray(a, BF16)
    return as_bf16(lower), as_bf16(sel_q), as_bf16(sel_k), jnp.asarray(ones_q), jnp.asarray(ones_k)


def _qkv_kernel(h_ref, wq_ref, wk_ref, wv_ref, wf_ref, bf_ref, qg_ref, kg_ref, lower_ref, selq_ref, selk_ref,
                oneq_ref, onek_ref, qa_ref, ka_ref, v_ref, carry_ref, *, tiles_per_seq):
    tm = h_ref.shape[0]

    @pl.when(pl.program_id(0) % tiles_per_seq == 0)
    def _():
        carry_ref[...] = jnp.zeros_like(carry_ref)

    h = h_ref[...]

    f_logit = _dot_nt(h, wf_ref[...]) + bf_ref[...]
    z_q = _dot_nt(h, wq_ref[...])
    lower = lower_ref[...]
    local = sum(_dot(lower, piece) for piece in _split_bf16(_log_sigmoid(f_logit)))
    z_k = _dot_nt(h, wk_ref[...])
    cum = local + carry_ref[...]
    carry_ref[...] = cum[tm - 1:tm, :]
    lane = lax.broadcasted_iota(jnp.int32, cum.shape, 1)
    pieces = _split_bf16(cum * (HEAD_DIM ** 0.5))
    for piece in range(1, N_SPLIT):
        pieces[0] = jnp.where(lane >= piece * HEADS, pieces[piece], pieces[0])
    pieces = pieces[0]
    decay_q = _dot(pieces, selq_ref[...])
    decay_k = _dot(pieces, selk_ref[...])
    z_v = _dot_nt(h, wv_ref[...])

    for z, decay, gain_ref, ones_ref, out_ref in ((z_q, decay_q, qg_ref, oneq_ref, qa_ref),
                                                  (z_k, decay_k, kg_ref, onek_ref, ka_ref)):
        for hd in range(HEADS):
            cols = slice(hd * HEAD_DIM, (hd + 1) * HEAD_DIM)
            out_ref[:, hd * AUG:hd * AUG + HEAD_DIM] = _rms(z[:, cols], gain_ref[...]).astype(BF16)
            out_ref[:, hd * AUG + HEAD_DIM:(hd + 1) * AUG] = (decay[:, cols] + ones_ref[...]).astype(BF16)
    v_ref[...] = z_v.astype(BF16)


def _qkv(h, w_qkv_t, w_f_t, b_f, q_g, k_g, tm, seq, casts):
    m = h.shape[0]
    row = lambda i: (i, 0)
    consts = _decay_constants(tm)
    (q_aug, k_aug, v), cast = _hosted_call(
        functools.partial(_qkv_kernel, tiles_per_seq=seq // tm),
        grid=(m // tm,),
        in_specs=[
            pl.BlockSpec((tm, D_MODEL), row),
            _resident((ATTN_WIDTH, D_MODEL), (0, 0)),
            _resident((ATTN_WIDTH, D_MODEL), (1, 0)),
            _resident((ATTN_WIDTH, D_MODEL), (2, 0)),
            _resident((LANES, D_MODEL)),
            _resident((1, LANES)),
            _resident((1, HEAD_DIM)),
            _resident((1, HEAD_DIM)),
        ] + [_resident(c.shape) for c in consts],
        out_specs=[
            pl.BlockSpec((tm, HEADS * AUG), row),
            pl.BlockSpec((tm, HEADS * AUG), row),
            pl.BlockSpec((tm, ATTN_WIDTH), row),
        ],
        out_shape=[
            jax.ShapeDtypeStruct((m, HEADS * AUG), BF16),
            jax.ShapeDtypeStruct((m, HEADS * AUG), BF16),
            jax.ShapeDtypeStruct((m, ATTN_WIDTH), BF16),
        ],
        scratch_shapes=[pltpu.VMEM((1, LANES), F32)],
        semantics=("arbitrary",),
        name="qkv",
        casts=casts,
    )(h, w_qkv_t, w_qkv_t, w_qkv_t, w_f_t, b_f, q_g, k_g, *consts)
    return q_aug, k_aug, v, cast


def _attn_kernel(q_ref, k_ref, v_ref, o_ref, *, tq):
    seq = q_ref.shape[0]
    to_log2 = (HEAD_DIM ** -0.5) * LOG2_E
    row = lax.broadcasted_iota(jnp.int32, (tq, tq), 0)
    col = lax.broadcasted_iota(jnp.int32, (tq, tq), 1)
    keep = row >= col
    v_aug = jnp.concatenate([v_ref[...], jnp.ones((seq, HEAD_DIM), BF16)], axis=1)
    for qi in reversed(range(seq // tq)):
        q0, q1 = qi * tq, (qi + 1) * tq
        q = q_ref[q0:q1, :]
        diag = jnp.where(keep, _dot_nt(q, k_ref[q0:q1, :]), -jnp.inf)
        m = jnp.max(diag, axis=-1, keepdims=True)
        if qi:
            past = _dot_nt(q, k_ref[:q0, :])
            m = jnp.maximum(m, jnp.max(past, axis=-1, keepdims=True))
        acc = _dot(jnp.exp2((diag - m) * to_log2).astype(BF16), v_aug[q0:q1, :])
        if qi:
            acc = acc + _dot(jnp.exp2((past - m) * to_log2).astype(BF16), v_aug[:q0, :])
        o_ref[q0:q1, :] = (acc[:, :HEAD_DIM] / acc[:, HEAD_DIM:]).astype(BF16)


def _attention(q_aug, k_aug, v, batch, seq, tq, casts):
    m = v.shape[0]
    head_block = lambda b, hd: (b, hd)
    (o_b,), cast = _hosted_call(
        functools.partial(_attn_kernel, tq=tq),
        grid=(batch, HEADS),
        in_specs=[
            pl.BlockSpec((seq, AUG), head_block),
            pl.BlockSpec((seq, AUG), head_block),
            pl.BlockSpec((seq, HEAD_DIM), head_block),
        ],
        out_specs=[pl.BlockSpec((seq, HEAD_DIM), head_block)],
        out_shape=[jax.ShapeDtypeStruct((m, ATTN_WIDTH), BF16)],
        semantics=("parallel", "parallel"),
        name="attention",
        casts=casts,
    )(q_aug, k_aug, v)
    return o_b, cast


def _merge_kernel(h_ref, oa_ref, ob_ref, wga_ref, wgb_ref, wba_ref, wbb_ref, y_ref):
    h = h_ref[...]
    ya = _sigmoid(_dot_nt(h, wga_ref[...])) * _dot(oa_ref[...], wba_ref[...])
    yb = _sigmoid(_dot_nt(h, wgb_ref[...])) * _dot(ob_ref[...], wbb_ref[...])
    y_ref[...] = (ya + yb).astype(BF16)


def _merge(h, o_a, o_b, w_g_t, w_ba, w_bb, tm, tn, casts):
    m = h.shape[0]
    nn = D_MODEL // tn
    row = lambda i, j: (i, 0)
    col = lambda i, j: (0, j)
    (y,), cast = _hosted_call(
        _merge_kernel,
        grid=(m // tm, nn),
        in_specs=[
            pl.BlockSpec((tm, D_MODEL), row),
            pl.BlockSpec((tm, GMLP_WIDTH), row),
            pl.BlockSpec((tm, ATTN_WIDTH), row),
            pl.BlockSpec((tn, D_MODEL), lambda i, j: (j, 0)),
            pl.BlockSpec((tn, D_MODEL), lambda i, j: (nn + j, 0)),
            pl.BlockSpec((GMLP_WIDTH, tn), col),
            pl.BlockSpec((ATTN_WIDTH, tn), col),
        ],
        out_specs=[pl.BlockSpec((tm, tn), lambda i, j: (i, j))],
        out_shape=[jax.ShapeDtypeStruct((m, D_MODEL), BF16)],
        semantics=("parallel", "arbitrary"),
        name="merge",
        casts=casts,
    )(h, o_a, o_b, w_g_t, w_g_t, w_ba, w_bb)
    return y, cast


def _outproj_kernel(x_hbm, y_ref, w_ref, g_ref, x1_ref, h2_ref, ring_ref, sem_ref, *, n_steps):
    tm = y_ref.shape[0]
    x_ref = _ring_fetch(x_hbm, ring_ref, sem_ref, tm, n_steps)

    def project(r0):
        return _dot(y_ref[r0:r0 + SUB_ROWS, :], w_ref[...])

    def finish(r0, z):
        sub = slice(r0, r0 + SUB_ROWS)
        x1 = x_ref[sub, :] + z
        x1_ref[sub, :] = x1
        h2_ref[sub, :] = _rms(x1, g_ref[...]).astype(BF16)

    _software_pipeline(range(0, tm, SUB_ROWS), project, finish)


def _outproj(x, y, w_out, g_next, tm):
    m = x.shape[0]
    row = lambda i: (i, 0)
    return pl.pallas_call(
        functools.partial(_outproj_kernel, n_steps=m // tm),
        grid=(m // tm,),
        in_specs=[
            pl.BlockSpec(memory_space=pl.ANY),
            pl.BlockSpec((tm, D_MODEL), row),
            _resident((D_MODEL, D_MODEL)),
            _resident((1, D_MODEL)),
        ],
        out_specs=[pl.BlockSpec((tm, D_MODEL), row), pl.BlockSpec((tm, D_MODEL), row)],
        out_shape=[jax.ShapeDtypeStruct((m, D_MODEL), F32), jax.ShapeDtypeStruct((m, D_MODEL), BF16)],
        scratch_shapes=[pltpu.VMEM((RING_SLOTS, tm, D_MODEL), F32), pltpu.SemaphoreType.DMA((RING_SLOTS,))],
        compiler_params=_params("arbitrary"),
        name="outproj",
    )(x, y, w_out, g_next)


def _ffn_up_kernel(h_ref, wa_ref, wb_ref, cw_ref, cb_ref, o_ref):
    seq = h_ref.shape[0]

    def proj(w):
        return jnp.concatenate([_dot(h_ref[r:r + DOT_ROWS, :], w) for r in range(0, seq, DOT_ROWS)], axis=0)

    a = proj(wa_ref[...])
    t = lax.broadcasted_iota(jnp.int32, a.shape, 0)
    a1 = jnp.where(t >= 1, pltpu.roll(a, 1, 0), 0.0)
    a2 = jnp.where(t >= 2, pltpu.roll(a, 2, 0), 0.0)
    conv = cb_ref[...] + a2 * cw_ref[0:1, :] + a1 * cw_ref[1:2, :] + a * cw_ref[2:3, :]
    o_ref[...] = (_gelu_tanh(conv) * proj(wb_ref[...])).astype(BF16)


def _ffn_up(h2, w_up_a, w_up_b, conv_w, conv_b, seq, tf, casts):
    m = h2.shape[0]
    col = lambda b, j: (0, j)
    (hidden,), cast = _hosted_call(
        _ffn_up_kernel,
        grid=(m // seq, D_FF // tf),
        in_specs=[
            pl.BlockSpec((seq, D_MODEL), lambda b, j: (b, 0)),
            pl.BlockSpec((D_MODEL, tf), col),
            pl.BlockSpec((D_MODEL, tf), col),
            pl.BlockSpec((conv_w.shape[0], tf), col),
            pl.BlockSpec((1, tf), col),
        ],
        out_specs=[pl.BlockSpec((seq, tf), lambda b, j: (b, j))],
        out_shape=[jax.ShapeDtypeStruct((m, D_FF), BF16)],
        semantics=("parallel", "arbitrary"),
        name="ffn_up",
        casts=casts,
    )(h2, w_up_a, w_up_b, conv_w, conv_b)
    return hidden, cast


def _ffn_down_kernel(x_ref, hid_ref, w_ref, g_ref, x2_ref, h3_ref):
    x2 = x_ref[...] + _dot(hid_ref[...], w_ref[...])
    x2_ref[...] = x2
    h3_ref[...] = _rms(x2, g_ref[...]).astype(BF16)


def _ffn_down(x1, hidden, w_down, g_next, tm, casts):
    m = x1.shape[0]
    row = lambda i: (i, 0)
    (x2, h3), cast = _hosted_call(
        _ffn_down_kernel,
        grid=(m // tm,),
        in_specs=[
            pl.BlockSpec((tm, D_MODEL), row),
            pl.BlockSpec((tm, D_FF), row),
            _resident((D_FF, D_MODEL)),
            _resident((1, D_MODEL)),
        ],
        out_specs=[pl.BlockSpec((tm, D_MODEL), row), pl.BlockSpec((tm, D_MODEL), row)],
        out_shape=[jax.ShapeDtypeStruct((m, D_MODEL), F32), jax.ShapeDtypeStruct((m, D_MODEL), BF16)],
        semantics=("parallel",),
        name="ffn_down",
        casts=casts,
    )(x1, hidden, w_down, g_next)
    return x2, h3, cast


def _ple_kernel(x_ref, h_ref, p_ref, wg_ref, wp_ref, g_ref, o_ref):
    w_proj = wp_ref[...].astype(BF16)

    def project(r0):
        sub = slice(r0, r0 + SUB_ROWS)
        return _dot(p_ref[sub, :].astype(BF16), w_proj), _dot(h_ref[sub, :], wg_ref[...])

    def finish(r0, z):
        sub = slice(r0, r0 + SUB_ROWS)
        o_ref[sub, :] = x_ref[sub, :] + _sigmoid(z[1]) * _rms(z[0], g_ref[...])

    _software_pipeline(range(0, x_ref.shape[0], SUB_ROWS), project, finish)


def _ple(x2, h3, p, w_gate, w_proj, g, tm):
    m = x2.shape[0]
    row = lambda i: (i, 0)
    return pl.pallas_call(
        _ple_kernel,
        grid=(m // tm,),
        in_specs=[
            pl.BlockSpec((tm, D_MODEL), row),
            pl.BlockSpec((tm, D_MODEL), row),
            pl.BlockSpec((tm, PLE_DIM), row),
            _resident((D_MODEL, D_MODEL)),
            _resident((PLE_DIM, D_MODEL)),
            _resident((1, D_MODEL)),
        ],
        out_specs=pl.BlockSpec((tm, D_MODEL), row),
        out_shape=jax.ShapeDtypeStruct((m, D_MODEL), F32),
        compiler_params=_params("parallel"),
        name="ple",
    )(x2, h3, p, w_gate, w_proj, g)


def _layer(x, p, norm_mix_g, w_in, gmlp_ln_g, gmlp_ln_b, gmlp_w_s, gmlp_b_s, fox_b_f, q_norm_g,
           k_norm_g, w_branch_a, w_branch_b, w_out, norm_ffn_g, w_up, conv_w, conv_b, w_down,
           ple_proj, ple_norm_g, ple_gate_norm_g, w_ple_gate, *, batch, seq):
    row_vec = lambda v: v.reshape(1, -1)
    whole = lambda w: CastJob(w, 0, w.shape[0], 0, w.shape[1])
    w_in_t = w_in.T
    w_uv_t, w_f_t = lax.optimization_barrier((w_in_t[:Q_COLS], w_in_t[F_COLS:G_COLS]))
    w_uv_t = w_uv_t.astype(BF16)
    w_f_t = jnp.pad(jnp.tile(w_f_t, (N_SPLIT, 1)), ((0, LANES - N_SPLIT * HEADS), (0, 0))).astype(BF16)
    b_f = jnp.pad(jnp.tile(fox_b_f, N_SPLIT), (0, LANES - N_SPLIT * HEADS)).reshape(1, LANES)

    o_a, h, (w_qkv_t,) = _gmlp(
        x, row_vec(norm_mix_g), w_uv_t, row_vec(gmlp_ln_g), row_vec(gmlp_ln_b), gmlp_w_s, gmlp_b_s.T,
        tm=TILES.gmlp_rows, casts=[CastJob(w_in_t, Q_COLS, 3 * ATTN_WIDTH, 0, D_MODEL)])
    q_aug, k_aug, v, (w_g_t, w_ba_bf, w_bb_bf) = _qkv(
        h, w_qkv_t, w_f_t, b_f, row_vec(q_norm_g), row_vec(k_norm_g), tm=TILES.qkv_rows, seq=seq,
        casts=[CastJob(w_in_t, G_COLS, 2 * D_MODEL, 0, D_MODEL), whole(w_branch_a), whole(w_branch_b)])
    o_b, (w_up_a,) = _attention(
        q_aug, k_aug, v, batch, seq, tq=TILES.attn_query_rows, casts=[CastJob(w_up, 0, D_MODEL, 0, D_FF)])
    y, (w_out_bf, w_up_b) = _merge(
        h, o_a, o_b, w_g_t, w_ba_bf, w_bb_bf, tm=TILES.merge_rows, tn=TILES.merge_cols,
        casts=[whole(w_out), CastJob(w_up, 0, D_MODEL, D_FF, D_FF)])
    x1, h2 = _outproj(x, y, w_out_bf, row_vec(norm_ffn_g), tm=TILES.outproj_rows)
    hidden, (w_down_bf,) = _ffn_up(h2, w_up_a, w_up_b, conv_w, row_vec(conv_b), seq=seq, tf=TILES.ffn_cols,
                                   casts=[whole(w_down)])
    x2, h3, (w_gate_bf,) = _ffn_down(x1, hidden, w_down_bf, row_vec(ple_gate_norm_g), tm=TILES.ffn_down_rows,
                                     casts=[whole(w_ple_gate)])
    return _ple(x2, h3, p, w_gate_bf, ple_proj, row_vec(ple_norm_g), tm=TILES.ple_rows)


def kernel(x, p, norm_mix_g, w_in, gmlp_ln_g, gmlp_ln_b, gmlp_w_s, gmlp_b_s, fox_b_f, q_norm_g, k_norm_g, w_branch_a, w_branch_b, w_out, norm_ffn_g, w_up, conv_w, conv_b, w_down, ple_proj, ple_norm_g, ple_gate_norm_g, w_ple_gate):
    batch, seq, d = x.shape
    depth = w_in.shape[0]
    xs = x.reshape(batch * seq, d)
    for i in range(depth):
        xs = _layer(xs, p[i].reshape(batch * seq, -1), norm_mix_g[i], w_in[i], gmlp_ln_g[i], gmlp_ln_b[i],
                    gmlp_w_s[i], gmlp_b_s[i], fox_b_f[i], q_norm_g[i], k_norm_g[i], w_branch_a[i],
                    w_branch_b[i], w_out[i], norm_ffn_g[i], w_up[i], conv_w[i], conv_b[i], w_down[i],
                    ple_proj[i], ple_norm_g[i], ple_gate_norm_g[i], w_ple_gate[i], batch=batch, seq=seq)
    return xs.reshape(batch, seq, d)
```

```python
import functools
import math
from typing import NamedTuple

import jax
import jax.numpy as jnp
import numpy as np
from jax import lax
from jax.experimental import pallas as pl
from jax.experimental.pallas import tpu as pltpu

D_MODEL = 2048
CHUNK = 128
GROUPS = 8
GROUP_DIM = 128
GMLP_WIDTH = GROUPS * GROUP_DIM
HEADS = 8
HEAD_DIM = 128
ATTN_WIDTH = HEADS * HEAD_DIM
D_FF = 5632
PLE_DIM = 256
EPS = 1e-6
Q_COLS = 2 * GMLP_WIDTH
F_COLS = Q_COLS + 3 * ATTN_WIDTH
G_COLS = F_COLS + HEADS
LANES = 128
F32_SUBLANES = 8
BF16_SUBLANES = 2 * F32_SUBLANES
AUG = 2 * HEAD_DIM
N_SPLIT = 3
LOG2_E = 1.4426950408889634
DOT_ROWS = 512
SUB_ROWS = 256

VMEM_LIMIT = 56 * 1024 * 1024


class Tiles(NamedTuple):
    gmlp_rows: int = 512
    qkv_rows: int = 512
    attn_query_rows: int = 512
    merge_rows: int = 1024
    merge_cols: int = 512
    outproj_rows: int = 512
    ffn_cols: int = 512
    ffn_down_rows: int = 256
    ple_rows: int = 512


TILES = Tiles()

F32 = jnp.float32
BF16 = jnp.bfloat16
NT = (((1,), (1,)), ((), ()))


def _params(*semantics):
    return pltpu.CompilerParams(dimension_semantics=semantics, vmem_limit_bytes=VMEM_LIMIT)


def _resident(shape, block=None):
    index = tuple(block) if block is not None else (0,) * len(shape)
    return pl.BlockSpec(shape, lambda *_: index, pipeline_mode=pl.Buffered(1))


def _dot(a, b):
    return jnp.dot(a, b, preferred_element_type=F32)


def _dot_nt(a, b):
    return lax.dot_general(a, b, NT, preferred_element_type=F32)


def _software_pipeline(items, produce, consume):
    pending = None
    for item in items:
        produced = produce(item)
        if pending is not None:
            consume(*pending)
        pending = (item, produced)
    consume(*pending)


def _rms(xf, g):
    return xf * lax.rsqrt(jnp.mean(xf * xf, axis=-1, keepdims=True) + EPS) * g


def _gelu_erf(x):
    return 0.5 * x * (1.0 + lax.erf(x * (2.0 ** -0.5)))


def _gelu_tanh(x):
    return 0.5 * x * (1.0 + jnp.tanh((2.0 / jnp.pi) ** 0.5 * (x + 0.044715 * (x * x * x))))


def _sigmoid(x):
    return 1.0 / (1.0 + jnp.exp(-x))


def _log_sigmoid(x):
    return -(jnp.maximum(-x, 0.0) + jnp.log1p(jnp.exp(-jnp.abs(x))))


class CastJob(NamedTuple):
    src: jax.Array
    row_start: int
    rows: int
    col_start: int
    cols: int


def _hosted_call(kernel_fn, *, grid, in_specs, out_specs, out_shape, semantics, name, casts=(),
                 scratch_shapes=()):
    n_in, n_out, n_cast = len(in_specs), len(out_specs), len(casts)
    n_steps = math.prod(grid)

    def step_of(*ids):
        step = ids[0]
        for extent, idx in zip(grid[1:], ids[1:]):
            step = step * extent + idx
        return step

    cast_in, cast_out, cast_shape = [], [], []
    for job in casts:
        slab, rem = divmod(job.rows, n_steps)
        assert rem == 0 and slab % BF16_SUBLANES == 0, (name, job[1:], n_steps)
        assert job.row_start % F32_SUBLANES == 0 and job.col_start % LANES == 0, (name, job[1:])
        cast_in.append(pl.BlockSpec(
            (pl.Element(slab), pl.Element(job.cols)),
            lambda *ids, job=job, slab=slab: (
                pl.multiple_of(job.row_start + step_of(*ids) * slab, F32_SUBLANES), job.col_start)))
        cast_out.append(pl.BlockSpec((slab, job.cols), lambda *ids: (step_of(*ids), 0)))
        cast_shape.append(jax.ShapeDtypeStruct((job.rows, job.cols), BF16))

    def body(*refs):
        ins, refs = refs[:n_in], refs[n_in:]
        srcs, refs = refs[:n_cast], refs[n_cast:]
        outs, refs = refs[:n_out], refs[n_out:]
        dsts, scratch = refs[:n_cast], refs[n_cast:]
        kernel_fn(*ins, *outs, *scratch)
        for src, dst in zip(srcs, dsts):
            dst[...] = src[...].astype(BF16)

    call = pl.pallas_call(
        body,
        grid=grid,
        in_specs=list(in_specs) + cast_in,
        out_specs=list(out_specs) + cast_out,
        out_shape=list(out_shape) + cast_shape,
        scratch_shapes=scratch_shapes,
        compiler_params=_params(*semantics),
        name=name,
    )

    def run(*operands):
        results = call(*operands, *(job.src for job in casts))
        return results[:n_out], results[n_out:]

    return run


def _gmlp_kernel(x_ref, g_ref, wf32_ref, lng_ref, lnb_ref, ws_ref, bst_ref, oa_ref, h_ref, w_ref):
    tm = x_ref.shape[0]

    @pl.when(pl.program_id(0) == 0)
    def _():
        w_ref[...] = wf32_ref[...].astype(BF16)

    row = lax.broadcasted_iota(jnp.int32, (CHUNK, CHUNK), 0)
    col = lax.broadcasted_iota(jnp.int32, (CHUNK, CHUNK), 1)
    causal = row >= col
    w_mix = [jnp.where(causal, ws_ref[grp], 0.0).astype(BF16) for grp in range(GROUPS)]

    def project(r0):
        sub = slice(r0, r0 + SUB_ROWS)
        h = _rms(x_ref[sub, :], g_ref[...]).astype(BF16)
        h_ref[sub, :] = h
        return _dot_nt(h, w_ref[...])

    def mix(r0, z):
        u = _gelu_erf(z[:, :GMLP_WIDTH])
        v = _gelu_erf(z[:, GMLP_WIDTH:])
        mu = jnp.mean(v, axis=-1, keepdims=True)
        vc = v - mu
        vn = vc * lax.rsqrt(jnp.mean(vc * vc, axis=-1, keepdims=True) + EPS)
        vn = (vn * lng_ref[...] + lnb_ref[...]).astype(BF16)
        for grp in range(GROUPS):
            cols = slice(grp * GROUP_DIM, (grp + 1) * GROUP_DIM)
            bias = bst_ref[:, grp:grp + 1]
            for c0 in range(0, SUB_ROWS, CHUNK):
                rows = slice(c0, c0 + CHUNK)
                mixed = _dot(w_mix[grp], vn[rows, cols]) + bias
                oa_ref[r0 + c0:r0 + c0 + CHUNK, cols] = (u[rows, cols] * mixed).astype(BF16)

    _software_pipeline(range(0, tm, SUB_ROWS), project, mix)


def _gmlp(x, g, w_in_t, ln_g, ln_b, w_s, b_s_t, tm, casts):
    m = x.shape[0]
    row = lambda i: (i, 0)
    (o_a, h), cast = _hosted_call(
        _gmlp_kernel,
        grid=(m // tm,),
        in_specs=[
            pl.BlockSpec((tm, D_MODEL), row),
            _resident((1, D_MODEL)),
            _resident((2 * GMLP_WIDTH, D_MODEL)),
            _resident((1, GMLP_WIDTH)),
            _resident((1, GMLP_WIDTH)),
            _resident((GROUPS, CHUNK, CHUNK)),
            _resident((CHUNK, GROUPS)),
        ],
        out_specs=[pl.BlockSpec((tm, GMLP_WIDTH), row), pl.BlockSpec((tm, D_MODEL), row)],
        out_shape=[jax.ShapeDtypeStruct((m, GMLP_WIDTH), BF16), jax.ShapeDtypeStruct((m, D_MODEL), BF16)],
        scratch_shapes=[pltpu.VMEM((2 * GMLP_WIDTH, D_MODEL), BF16)],
        semantics=("arbitrary",),
        name="gmlp",
        casts=casts,
    )(x, g, w_in_t, ln_g, ln_b, w_s, b_s_t)
    return o_a, h, cast


def _split_bf16(x):
    pieces = []
    for _ in range(N_SPLIT):
        piece = x.astype(BF16)
        pieces.append(piece)
        x = x - piece.astype(F32)
    return pieces


def _decay_constants(tm):
    lower = np.tril(np.ones((tm, tm), np.float32))
    sel_q = np.zeros((LANES, HEADS * HEAD_DIM), np.float32)
    sel_k = np.zeros_like(sel_q)
    ones_q = np.zeros((1, HEAD_DIM), np.float32)
    ones_k = np.zeros((1, HEAD_DIM), np.float32)
    for piece in range(N_SPLIT):
        ones_q[0, N_SPLIT + piece] = 1.0
        ones_k[0, piece] = 1.0
        for head in range(HEADS):
            sel_q[piece * HEADS + head, head * HEAD_DIM + piece] = 1.0
            sel_k[piece * HEADS + head, head * HEAD_DIM + N_SPLIT + piece] = -1.0
    as_bf16 = lambda a: jnp.asarray(a, BF16)
    return as_bf16(lower), as_bf16(sel_q), as_bf16(sel_k), jnp.asarray(ones_q), jnp.asarray(ones_k)


def _qkv_kernel(h_ref, wq_ref, wk_ref, wv_ref, wf_ref, bf_ref, qg_ref, kg_ref, lower_ref, selq_ref, selk_ref,
                oneq_ref, onek_ref, qa_ref, ka_ref, v_ref, carry_ref, *, tiles_per_seq):
    tm = h_ref.shape[0]

    @pl.when(pl.program_id(0) % tiles_per_seq == 0)
    def _():
        carry_ref[...] = jnp.zeros_like(carry_ref)

    h = h_ref[...]

    w_f = jnp.concatenate([wf_ref[...]] * N_SPLIT + [jnp.zeros((LANES - N_SPLIT * HEADS, D_MODEL), F32)], axis=0)
    f_logit = _dot_nt(h, w_f.astype(BF16)) + bf_ref[...]
    z_q = _dot_nt(h, wq_ref[...])
    lower = lower_ref[...]
    local = sum(_dot(lower, piece) for piece in _split_bf16(_log_sigmoid(f_logit)))
    z_k = _dot_nt(h, wk_ref[...])
    cum = local + carry_ref[...]
    carry_ref[...] = cum[tm - 1:tm, :]
    lane = lax.broadcasted_iota(jnp.int32, cum.shape, 1)
    pieces = _split_bf16(cum * (HEAD_DIM ** 0.5))
    for piece in range(1, N_SPLIT):
        pieces[0] = jnp.where(lane >= piece * HEADS, pieces[piece], pieces[0])
    pieces = pieces[0]
    decay_q = _dot(pieces, selq_ref[...])
    decay_k = _dot(pieces, selk_ref[...])
    z_v = _dot_nt(h, wv_ref[...])

    for z, decay, gain_ref, ones_ref, out_ref in ((z_q, decay_q, qg_ref, oneq_ref, qa_ref),
                                                  (z_k, decay_k, kg_ref, onek_ref, ka_ref)):
        for hd in range(HEADS):
            cols = slice(hd * HEAD_DIM, (hd + 1) * HEAD_DIM)
            out_ref[:, hd * AUG:hd * AUG + HEAD_DIM] = _rms(z[:, cols], gain_ref[...]).astype(BF16)
            out_ref[:, hd * AUG + HEAD_DIM:(hd + 1) * AUG] = (decay[:, cols] + ones_ref[...]).astype(BF16)
    v_ref[...] = z_v.astype(BF16)


def _qkv(h, w_qkv_t, w_in_t, b_f, q_g, k_g, tm, seq, casts):
    m = h.shape[0]
    row = lambda i: (i, 0)
    consts = _decay_constants(tm)
    (q_aug, k_aug, v), cast = _hosted_call(
        functools.partial(_qkv_kernel, tiles_per_seq=seq // tm),
        grid=(m // tm,),
        in_specs=[
            pl.BlockSpec((tm, D_MODEL), row),
            _resident((ATTN_WIDTH, D_MODEL), (0, 0)),
            _resident((ATTN_WIDTH, D_MODEL), (1, 0)),
            _resident((ATTN_WIDTH, D_MODEL), (2, 0)),
            _resident((HEADS, D_MODEL), (F_COLS // HEADS, 0)),
            _resident((1, LANES)),
            _resident((1, HEAD_DIM)),
            _resident((1, HEAD_DIM)),
        ] + [_resident(c.shape) for c in consts],
        out_specs=[
            pl.BlockSpec((tm, HEADS * AUG), row),
            pl.BlockSpec((tm, HEADS * AUG), row),
            pl.BlockSpec((tm, ATTN_WIDTH), row),
        ],
        out_shape=[
            jax.ShapeDtypeStruct((m, HEADS * AUG), BF16),
            jax.ShapeDtypeStruct((m, HEADS * AUG), BF16),
            jax.ShapeDtypeStruct((m, ATTN_WIDTH), BF16),
        ],
        scratch_shapes=[pltpu.VMEM((1, LANES), F32)],
        semantics=("arbitrary",),
        name="qkv",
        casts=casts,
    )(h, w_qkv_t, w_qkv_t, w_qkv_t, w_in_t, b_f, q_g, k_g, *consts)
    return q_aug, k_aug, v, cast


def _attn_kernel(q_ref, k_ref, v_ref, o_ref, *, tq):
    seq = q_ref.shape[0]
    to_log2 = (HEAD_DIM ** -0.5) * LOG2_E
    row = lax.broadcasted_iota(jnp.int32, (tq, tq), 0)
    col = lax.broadcasted_iota(jnp.int32, (tq, tq), 1)
    keep = row >= col
    v_aug = jnp.concatenate([v_ref[...], jnp.ones((seq, HEAD_DIM), BF16)], axis=1)
    for qi in reversed(range(seq // tq)):
        q0, q1 = qi * tq, (qi + 1) * tq
        q = q_ref[q0:q1, :]
        diag = jnp.where(keep, _dot_nt(q, k_ref[q0:q1, :]), -jnp.inf)
        m = jnp.max(diag, axis=-1, keepdims=True)
        if qi:
            past = _dot_nt(q, k_ref[:q0, :])
            m = jnp.maximum(m, jnp.max(past, axis=-1, keepdims=True))
        acc = _dot(jnp.exp2((diag - m) * to_log2).astype(BF16), v_aug[q0:q1, :])
        if qi:
            acc = acc + _dot(jnp.exp2((past - m) * to_log2).astype(BF16), v_aug[:q0, :])
        o_ref[q0:q1, :] = (acc[:, :HEAD_DIM] / acc[:, HEAD_DIM:]).astype(BF16)


def _attention(q_aug, k_aug, v, batch, seq, tq, casts):
    m = v.shape[0]
    head_block = lambda b, hd: (b, hd)
    (o_b,), cast = _hosted_call(
        functools.partial(_attn_kernel, tq=tq),
        grid=(batch, HEADS),
        in_specs=[
            pl.BlockSpec((seq, AUG), head_block),
            pl.BlockSpec((seq, AUG), head_block),
            pl.BlockSpec((seq, HEAD_DIM), head_block),
        ],
        out_specs=[pl.BlockSpec((seq, HEAD_DIM), head_block)],
        out_shape=[jax.ShapeDtypeStruct((m, ATTN_WIDTH), BF16)],
        semantics=("parallel", "parallel"),
        name="attention",
        casts=casts,
    )(q_aug, k_aug, v)
    return o_b, cast


def _merge_kernel(h_ref, oa_ref, ob_ref, wga_ref, wgb_ref, wba_ref, wbb_ref, y_ref):
    h = h_ref[...]
    ya = _sigmoid(_dot_nt(h, wga_ref[...])) * _dot(oa_ref[...], wba_ref[...])
    yb = _sigmoid(_dot_nt(h, wgb_ref[...])) * _dot(ob_ref[...], wbb_ref[...])
    y_ref[...] = (ya + yb).astype(BF16)


def _merge(h, o_a, o_b, w_g_t, w_ba, w_bb, tm, tn, casts):
    m = h.shape[0]
    nn = D_MODEL // tn
    row = lambda i, j: (i, 0)
    col = lambda i, j: (0, j)
    (y,), cast = _hosted_call(
        _merge_kernel,
        grid=(m // tm, nn),
        in_specs=[
            pl.BlockSpec((tm, D_MODEL), row),
            pl.BlockSpec((tm, GMLP_WIDTH), row),
            pl.BlockSpec((tm, ATTN_WIDTH), row),
            pl.BlockSpec((tn, D_MODEL), lambda i, j: (j, 0)),
            pl.BlockSpec((tn, D_MODEL), lambda i, j: (nn + j, 0)),
            pl.BlockSpec((GMLP_WIDTH, tn), col),
            pl.BlockSpec((ATTN_WIDTH, tn), col),
        ],
        out_specs=[pl.BlockSpec((tm, tn), lambda i, j: (i, j))],
        out_shape=[jax.ShapeDtypeStruct((m, D_MODEL), BF16)],
        semantics=("parallel", "arbitrary"),
        name="merge",
        casts=casts,
    )(h, o_a, o_b, w_g_t, w_g_t, w_ba, w_bb)
    return y, cast


def _outproj_kernel(x_ref, y_ref, w_ref, g_ref, x1_ref, h2_ref):
    def project(r0):
        return _dot(y_ref[r0:r0 + SUB_ROWS, :], w_ref[...])

    def finish(r0, z):
        sub = slice(r0, r0 + SUB_ROWS)
        x1 = x_ref[sub, :] + z
        x1_ref[sub, :] = x1
        h2_ref[sub, :] = _rms(x1, g_ref[...]).astype(BF16)

    _software_pipeline(range(0, x_ref.shape[0], SUB_ROWS), project, finish)


def _outproj(x, y, w_out, g_next, tm):
    m = x.shape[0]
    row = lambda i: (i, 0)
    return pl.pallas_call(
        _outproj_kernel,
        grid=(m // tm,),
        in_specs=[
            pl.BlockSpec((tm, D_MODEL), row),
            pl.BlockSpec((tm, D_MODEL), row),
            _resident((D_MODEL, D_MODEL)),
            _resident((1, D_MODEL)),
        ],
        out_specs=[pl.BlockSpec((tm, D_MODEL), row), pl.BlockSpec((tm, D_MODEL), row)],
        out_shape=[jax.ShapeDtypeStruct((m, D_MODEL), F32), jax.ShapeDtypeStruct((m, D_MODEL), BF16)],
        compiler_params=_params("parallel"),
        name="outproj",
    )(x, y, w_out, g_next)


def _ffn_up_kernel(h_ref, wa_ref, wb_ref, cw_ref, cb_ref, o_ref):
    seq = h_ref.shape[0]

    def proj(w):
        return jnp.concatenate([_dot(h_ref[r:r + DOT_ROWS, :], w) for r in range(0, seq, DOT_ROWS)], axis=0)

    a = proj(wa_ref[...])
    t = lax.broadcasted_iota(jnp.int32, a.shape, 0)
    a1 = jnp.where(t >= 1, pltpu.roll(a, 1, 0), 0.0)
    a2 = jnp.where(t >= 2, pltpu.roll(a, 2, 0), 0.0)
    conv = cb_ref[...] + a2 * cw_ref[0:1, :] + a1 * cw_ref[1:2, :] + a * cw_ref[2:3, :]
    o_ref[...] = (_gelu_tanh(conv) * proj(wb_ref[...])).astype(BF16)


def _ffn_up(h2, w_up_a, w_up_b, conv_w, conv_b, seq, tf, casts):
    m = h2.shape[0]
    col = lambda b, j: (0, j)
    (hidden,), cast = _hosted_call(
        _ffn_up_kernel,
        grid=(m // seq, D_FF // tf),
        in_specs=[
            pl.BlockSpec((seq, D_MODEL), lambda b, j: (b, 0)),
            pl.BlockSpec((D_MODEL, tf), col),
            pl.BlockSpec((D_MODEL, tf), col),
            pl.BlockSpec((conv_w.shape[0], tf), col),
            pl.BlockSpec((1, tf), col),
        ],
        out_specs=[pl.BlockSpec((seq, tf), lambda b, j: (b, j))],
        out_shape=[jax.ShapeDtypeStruct((m, D_FF), BF16)],
        semantics=("parallel", "arbitrary"),
        name="ffn_up",
        casts=casts,
    )(h2, w_up_a, w_up_b, conv_w, conv_b)
    return hidden, cast


def _ffn_down_kernel(x_ref, hid_ref, w_ref, g_ref, x2_ref, h3_ref):
    x2 = x_ref[...] + _dot(hid_ref[...], w_ref[...])
    x2_ref[...] = x2
    h3_ref[...] = _rms(x2, g_ref[...]).astype(BF16)


def _ffn_down(x1, hidden, w_down, g_next, tm, casts):
    m = x1.shape[0]
    row = lambda i: (i, 0)
    (x2, h3), cast = _hosted_call(
        _ffn_down_kernel,
        grid=(m // tm,),
        in_specs=[
            pl.BlockSpec((tm, D_MODEL), row),
            pl.BlockSpec((tm, D_FF), row),
            _resident((D_FF, D_MODEL)),
            _resident((1, D_MODEL)),
        ],
        out_specs=[pl.BlockSpec((tm, D_MODEL), row), pl.BlockSpec((tm, D_MODEL), row)],
        out_shape=[jax.ShapeDtypeStruct((m, D_MODEL), F32), jax.ShapeDtypeStruct((m, D_MODEL), BF16)],
        semantics=("parallel",),
        name="ffn_down",
        casts=casts,
    )(x1, hidden, w_down, g_next)
    return x2, h3, cast


def _ple_kernel(x_ref, h_ref, p_ref, wg_ref, wp_ref, g_ref, o_ref):
    w_proj = wp_ref[...].astype(BF16)

    def project(r0):
        sub = slice(r0, r0 + SUB_ROWS)
        return _dot(p_ref[sub, :].astype(BF16), w_proj), _dot(h_ref[sub, :], wg_ref[...])

    def finish(r0, z):
        sub = slice(r0, r0 + SUB_ROWS)
        o_ref[sub, :] = x_ref[sub, :] + _sigmoid(z[1]) * _rms(z[0], g_ref[...])

    _software_pipeline(range(0, x_ref.shape[0], SUB_ROWS), project, finish)


def _ple(x2, h3, p, w_gate, w_proj, g, tm):
    m = x2.shape[0]
    row = lambda i: (i, 0)
    return pl.pallas_call(
        _ple_kernel,
        grid=(m // tm,),
        in_specs=[
            pl.BlockSpec((tm, D_MODEL), row),
            pl.BlockSpec((tm, D_MODEL), row),
            pl.BlockSpec((tm, PLE_DIM), row),
            _resident((D_MODEL, D_MODEL)),
            _resident((PLE_DIM, D_MODEL)),
            _resident((1, D_MODEL)),
        ],
        out_specs=pl.BlockSpec((tm, D_MODEL), row),
        out_shape=jax.ShapeDtypeStruct((m, D_MODEL), F32),
        compiler_params=_params("parallel"),
        name="ple",
    )(x2, h3, p, w_gate, w_proj, g)


def _layer(x, p, norm_mix_g, w_in, gmlp_ln_g, gmlp_ln_b, gmlp_w_s, gmlp_b_s, fox_b_f, q_norm_g,
           k_norm_g, w_branch_a, w_branch_b, w_out, norm_ffn_g, w_up, conv_w, conv_b, w_down,
           ple_proj, ple_norm_g, ple_gate_norm_g, w_ple_gate, *, batch, seq):
    row_vec = lambda v: v.reshape(1, -1)
    whole = lambda w: CastJob(w, 0, w.shape[0], 0, w.shape[1])
    w_in_t = w_in.T
    b_f = jnp.pad(jnp.tile(fox_b_f, N_SPLIT), (0, LANES - N_SPLIT * HEADS)).reshape(1, LANES)

    o_a, h, (w_qkv_t,) = _gmlp(
        x, row_vec(norm_mix_g), w_in_t, row_vec(gmlp_ln_g), row_vec(gmlp_ln_b), gmlp_w_s, gmlp_b_s.T,
        tm=TILES.gmlp_rows, casts=[CastJob(w_in_t, Q_COLS, 3 * ATTN_WIDTH, 0, D_MODEL)])
    q_aug, k_aug, v, (w_g_t, w_ba_bf, w_bb_bf) = _qkv(
        h, w_qkv_t, w_in_t, b_f, row_vec(q_norm_g), row_vec(k_norm_g), tm=TILES.qkv_rows, seq=seq,
        casts=[CastJob(w_in_t, G_COLS, 2 * D_MODEL, 0, D_MODEL), whole(w_branch_a), whole(w_branch_b)])
    o_b, (w_up_a,) = _attention(
        q_aug, k_aug, v, batch, seq, tq=TILES.attn_query_rows, casts=[CastJob(w_up, 0, D_MODEL, 0, D_FF)])
    y, (w_out_bf, w_up_b) = _merge(
        h, o_a, o_b, w_g_t, w_ba_bf, w_bb_bf, tm=TILES.merge_rows, tn=TILES.merge_cols,
        casts=[whole(w_out), CastJob(w_up, 0, D_MODEL, D_FF, D_FF)])
    x1, h2 = _outproj(x, y, w_out_bf, row_vec(norm_ffn_g), tm=TILES.outproj_rows)
    hidden, (w_down_bf,) = _ffn_up(h2, w_up_a, w_up_b, conv_w, row_vec(conv_b), seq=seq, tf=TILES.ffn_cols,
                                   casts=[whole(w_down)])
    x2, h3, (w_gate_bf,) = _ffn_down(x1, hidden, w_down_bf, row_vec(ple_gate_norm_g), tm=TILES.ffn_down_rows,
                                     casts=[whole(w_ple_gate)])
    return _ple(x2, h3, p, w_gate_bf, ple_proj, row_vec(ple_norm_g), tm=TILES.ple_rows)


def kernel(x, p, norm_mix_g, w_in, gmlp_ln_g, gmlp_ln_b, gmlp_w_s, gmlp_b_s, fox_b_f, q_norm_g, k_norm_g, w_branch_a, w_branch_b, w_out, norm_ffn_g, w_up, conv_w, conv_b, w_down, ple_proj, ple_norm_g, ple_gate_norm_g, w_ple_gate):
    batch, seq, d = x.shape
    depth = w_in.shape[0]
    xs = x.reshape(batch * seq, d)
    for i in range(depth):
        xs = _layer(xs, p[i].reshape(batch * seq, -1), norm_mix_g[i], w_in[i], gmlp_ln_g[i], gmlp_ln_b[i],
                    gmlp_w_s[i], gmlp_b_s[i], fox_b_f[i], q_norm_g[i], k_norm_g[i], w_branch_a[i],
                    w_branch_b[i], w_out[i], norm_ffn_g[i], w_up[i], conv_w[i], conv_b[i], w_down[i],
                    ple_proj[i], ple_norm_g[i], ple_gate_norm_g[i], w_ple_gate[i], batch=batch, seq=seq)
    return xs.reshape(batch, seq, d)
```

```python
import functools
import math
from typing import NamedTuple

import jax
import jax.numpy as jnp
import numpy as np
from jax import lax
from jax.experimental import pallas as pl
from jax.experimental.pallas import tpu as pltpu

D_MODEL = 2048
CHUNK = 128
GROUPS = 8
GROUP_DIM = 128
GMLP_WIDTH = GROUPS * GROUP_DIM
HEADS = 8
HEAD_DIM = 128
ATTN_WIDTH = HEADS * HEAD_DIM
D_FF = 5632
PLE_DIM = 256
EPS = 1e-6
Q_COLS = 2 * GMLP_WIDTH
F_COLS = Q_COLS + 3 * ATTN_WIDTH
G_COLS = F_COLS + HEADS
LANES = 128
F32_SUBLANES = 8
BF16_SUBLANES = 2 * F32_SUBLANES
AUG = 2 * HEAD_DIM
N_SPLIT = 3
LOG2_E = 1.4426950408889634
DOT_ROWS = 512
SUB_ROWS = 256

VMEM_LIMIT = 56 * 1024 * 1024


class Tiles(NamedTuple):
    gmlp_rows: int = 512
    qkv_rows: int = 512
    attn_query_rows: int = 512
    merge_rows: int = 1024
    merge_cols: int = 512
    outproj_rows: int = 512
    ffn_cols: int = 512
    ffn_down_rows: int = 256
    ple_rows: int = 512


TILES = Tiles()

F32 = jnp.float32
BF16 = jnp.bfloat16
NT = (((1,), (1,)), ((), ()))


def _params(*semantics):
    return pltpu.CompilerParams(dimension_semantics=semantics, vmem_limit_bytes=VMEM_LIMIT)


def _resident(shape, block=None):
    index = tuple(block) if block is not None else (0,) * len(shape)
    return pl.BlockSpec(shape, lambda *_: index, pipeline_mode=pl.Buffered(1))


def _dot(a, b):
    return jnp.dot(a, b, preferred_element_type=F32)


def _dot_nt(a, b):
    return lax.dot_general(a, b, NT, preferred_element_type=F32)


def _software_pipeline(items, produce, consume):
    pending = None
    for item in items:
        produced = produce(item)
        if pending is not None:
            consume(*pending)
        pending = (item, produced)
    consume(*pending)


def _rms(xf, g):
    return xf * lax.rsqrt(jnp.mean(xf * xf, axis=-1, keepdims=True) + EPS) * g


def _gelu_erf(x):
    return 0.5 * x * (1.0 + lax.erf(x * (2.0 ** -0.5)))


def _gelu_tanh(x):
    return 0.5 * x * (1.0 + jnp.tanh((2.0 / jnp.pi) ** 0.5 * (x + 0.044715 * (x * x * x))))


def _sigmoid(x):
    return 1.0 / (1.0 + jnp.exp(-x))


def _log_sigmoid(x):
    return -(jnp.maximum(-x, 0.0) + jnp.log1p(jnp.exp(-jnp.abs(x))))


class CastJob(NamedTuple):
    src: jax.Array
    row_start: int
    rows: int
    col_start: int
    cols: int


def _hosted_call(kernel_fn, *, grid, in_specs, out_specs, out_shape, semantics, name, casts=(),
                 scratch_shapes=()):
    n_in, n_out, n_cast = len(in_specs), len(out_specs), len(casts)
    n_steps = math.prod(grid)

    def step_of(*ids):
        step = ids[0]
        for extent, idx in zip(grid[1:], ids[1:]):
            step = step * extent + idx
        return step

    cast_in, cast_out, cast_shape = [], [], []
    for job in casts:
        slab, rem = divmod(job.rows, n_steps)
        assert rem == 0 and slab % BF16_SUBLANES == 0, (name, job[1:], n_steps)
        assert job.row_start % F32_SUBLANES == 0 and job.col_start % LANES == 0, (name, job[1:])
        cast_in.append(pl.BlockSpec(
            (pl.Element(slab), pl.Element(job.cols)),
            lambda *ids, job=job, slab=slab: (
                pl.multiple_of(job.row_start + step_of(*ids) * slab, F32_SUBLANES), job.col_start)))
        cast_out.append(pl.BlockSpec((slab, job.cols), lambda *ids: (step_of(*ids), 0)))
        cast_shape.append(jax.ShapeDtypeStruct((job.rows, job.cols), BF16))

    def body(*refs):
        ins, refs = refs[:n_in], refs[n_in:]
        srcs, refs = refs[:n_cast], refs[n_cast:]
        outs, refs = refs[:n_out], refs[n_out:]
        dsts, scratch = refs[:n_cast], refs[n_cast:]
        kernel_fn(*ins, *outs, *scratch)
        for src, dst in zip(srcs, dsts):
            dst[...] = src[...].astype(BF16)

    call = pl.pallas_call(
        body,
        grid=grid,
        in_specs=list(in_specs) + cast_in,
        out_specs=list(out_specs) + cast_out,
        out_shape=list(out_shape) + cast_shape,
        scratch_shapes=scratch_shapes,
        compiler_params=_params(*semantics),
        name=name,
    )

    def run(*operands):
        results = call(*operands, *(job.src for job in casts))
        return results[:n_out], results[n_out:]

    return run


def _gmlp_kernel(x_ref, g_ref, wf32_ref, lng_ref, lnb_ref, ws_ref, bst_ref, oa_ref, h_ref, w_ref):
    tm = x_ref.shape[0]

    @pl.when(pl.program_id(0) == 0)
    def _():
        w_ref[...] = wf32_ref[...].astype(BF16)

    row = lax.broadcasted_iota(jnp.int32, (CHUNK, CHUNK), 0)
    col = lax.broadcasted_iota(jnp.int32, (CHUNK, CHUNK), 1)
    causal = row >= col
    w_mix = [jnp.where(causal, ws_ref[grp], 0.0).astype(BF16) for grp in range(GROUPS)]

    def project(r0):
        sub = slice(r0, r0 + SUB_ROWS)
        h = _rms(x_ref[sub, :], g_ref[...]).astype(BF16)
        h_ref[sub, :] = h
        return _dot_nt(h, w_ref[...])

    def mix(r0, z):
        u = _gelu_erf(z[:, :GMLP_WIDTH])
        v = _gelu_erf(z[:, GMLP_WIDTH:])
        mu = jnp.mean(v, axis=-1, keepdims=True)
        vc = v - mu
        vn = vc * lax.rsqrt(jnp.mean(vc * vc, axis=-1, keepdims=True) + EPS)
        vn = (vn * lng_ref[...] + lnb_ref[...]).astype(BF16)
        for grp in range(GROUPS):
            cols = slice(grp * GROUP_DIM, (grp + 1) * GROUP_DIM)
            bias = bst_ref[:, grp:grp + 1]
            for c0 in range(0, SUB_ROWS, CHUNK):
                rows = slice(c0, c0 + CHUNK)
                mixed = _dot(w_mix[grp], vn[rows, cols]) + bias
                oa_ref[r0 + c0:r0 + c0 + CHUNK, cols] = (u[rows, cols] * mixed).astype(BF16)

    _software_pipeline(range(0, tm, SUB_ROWS), project, mix)


def _gmlp(x, g, w_in_t, ln_g, ln_b, w_s, b_s_t, tm, casts):
    m = x.shape[0]
    row = lambda i: (i, 0)
    (o_a, h), cast = _hosted_call(
        _gmlp_kernel,
        grid=(m // tm,),
        in_specs=[
            pl.BlockSpec((tm, D_MODEL), row),
            _resident((1, D_MODEL)),
            _resident((2 * GMLP_WIDTH, D_MODEL)),
            _resident((1, GMLP_WIDTH)),
            _resident((1, GMLP_WIDTH)),
            _resident((GROUPS, CHUNK, CHUNK)),
            _resident((CHUNK, GROUPS)),
        ],
        out_specs=[pl.BlockSpec((tm, GMLP_WIDTH), row), pl.BlockSpec((tm, D_MODEL), row)],
        out_shape=[jax.ShapeDtypeStruct((m, GMLP_WIDTH), BF16), jax.ShapeDtypeStruct((m, D_MODEL), BF16)],
        scratch_shapes=[pltpu.VMEM((2 * GMLP_WIDTH, D_MODEL), BF16)],
        semantics=("arbitrary",),
        name="gmlp",
        casts=casts,
    )(x, g, w_in_t, ln_g, ln_b, w_s, b_s_t)
    return o_a, h, cast


def _split_bf16(x):
    pieces = []
    for _ in range(N_SPLIT):
        piece = x.astype(BF16)
        pieces.append(piece)
        x = x - piece.astype(F32)
    return pieces


def _decay_constants(tm):
    lower = np.tril(np.ones((tm, tm), np.float32))
    sel_q = np.zeros((LANES, HEADS * HEAD_DIM), np.float32)
    sel_k = np.zeros_like(sel_q)
    ones_q = np.zeros((1, HEAD_DIM), np.float32)
    ones_k = np.zeros((1, HEAD_DIM), np.float32)
    for piece in range(N_SPLIT):
        ones_q[0, N_SPLIT + piece] = 1.0
        ones_k[0, piece] = 1.0
        for head in range(HEADS):
            sel_q[piece * HEADS + head, head * HEAD_DIM + piece] = 1.0
            sel_k[piece * HEADS + head, head * HEAD_DIM + N_SPLIT + piece] = -1.0
    as_bf16 = lambda a: jnp.asarray(a, BF16)
    return as_bf16(lower), as_bf16(sel_q), as_bf16(sel_k), jnp.asarray(ones_q), jnp.asarray(ones_k)


def _qkv_kernel(h_ref, wq_ref, wk_ref, wv_ref, wf_ref, bf_ref, qg_ref, kg_ref, lower_ref, selq_ref, selk_ref,
                oneq_ref, onek_ref, qa_ref, ka_ref, v_ref, carry_ref, *, tiles_per_seq):
    tm = h_ref.shape[0]

    @pl.when(pl.program_id(0) % tiles_per_seq == 0)
    def _():
        carry_ref[...] = jnp.zeros_like(carry_ref)

    h = h_ref[...]

    w_f = jnp.concatenate([wf_ref[...]] * N_SPLIT + [jnp.zeros((LANES - N_SPLIT * HEADS, D_MODEL), F32)], axis=0)
    f_logit = _dot_nt(h, w_f.astype(BF16)) + bf_ref[...]
    z_q = _dot_nt(h, wq_ref[...])
    lower = lower_ref[...]
    local = sum(_dot(lower, piece) for piece in _split_bf16(_log_sigmoid(f_logit)))
    z_k = _dot_nt(h, wk_ref[...])
    cum = local + carry_ref[...]
    carry_ref[...] = cum[tm - 1:tm, :]
    lane = lax.broadcasted_iota(jnp.int32, cum.shape, 1)
    pieces = _split_bf16(cum * (HEAD_DIM ** 0.5))
    for piece in range(1, N_SPLIT):
        pieces[0] = jnp.where(lane >= piece * HEADS, pieces[piece], pieces[0])
    pieces = pieces[0]
    decay_q = _dot(pieces, selq_ref[...])
    decay_k = _dot(pieces, selk_ref[...])
    z_v = _dot_nt(h, wv_ref[...])

    for z, decay, gain_ref, ones_ref, out_ref in ((z_q, decay_q, qg_ref, oneq_ref, qa_ref),
                                                  (z_k, decay_k, kg_ref, onek_ref, ka_ref)):
        for hd in range(HEADS):
            cols = slice(hd * HEAD_DIM, (hd + 1) * HEAD_DIM)
            out_ref[:, hd * AUG:hd * AUG + HEAD_DIM] = _rms(z[:, cols], gain_ref[...]).astype(BF16)
            out_ref[:, hd * AUG + HEAD_DIM:(hd + 1) * AUG] = (decay[:, cols] + ones_ref[...]).astype(BF16)
    v_ref[...] = z_v.astype(BF16)


def _qkv(h, w_qkv_t, w_in_t, b_f, q_g, k_g, tm, seq, casts):
    m = h.shape[0]
    row = lambda i: (i, 0)
    consts = _decay_constants(tm)
    (q_aug, k_aug, v), cast = _hosted_call(
        functools.partial(_qkv_kernel, tiles_per_seq=seq // tm),
        grid=(m // tm,),
        in_specs=[
            pl.BlockSpec((tm, D_MODEL), row),
            _resident((ATTN_WIDTH, D_MODEL), (0, 0)),
            _resident((ATTN_WIDTH, D_MODEL), (1, 0)),
            _resident((ATTN_WIDTH, D_MODEL), (2, 0)),
            _resident((HEADS, D_MODEL), (F_COLS // HEADS, 0)),
            _resident((1, LANES)),
            _resident((1, HEAD_DIM)),
            _resident((1, HEAD_DIM)),
        ] + [_resident(c.shape) for c in consts],
        out_specs=[
            pl.BlockSpec((tm, HEADS * AUG), row),
            pl.BlockSpec((tm, HEADS * AUG), row),
            pl.BlockSpec((tm, ATTN_WIDTH), row),
        ],
        out_shape=[
            jax.ShapeDtypeStruct((m, HEADS * AUG), BF16),
            jax.ShapeDtypeStruct((m, HEADS * AUG), BF16),
            jax.ShapeDtypeStruct((m, ATTN_WIDTH), BF16),
        ],
        scratch_shapes=[pltpu.VMEM((1, LANES), F32)],
        semantics=("arbitrary",),
        name="qkv",
        casts=casts,
    )(h, w_qkv_t, w_qkv_t, w_qkv_t, w_in_t, b_f, q_g, k_g, *consts)
    return q_aug, k_aug, v, cast


def _attn_kernel(q_ref, k_ref, v_ref, o_ref, *, tq):
    seq = q_ref.shape[0]
    to_log2 = (HEAD_DIM ** -0.5) * LOG2_E
    row = lax.broadcasted_iota(jnp.int32, (tq, tq), 0)
    col = lax.broadcasted_iota(jnp.int32, (tq, tq), 1)
    keep = row >= col
    v_aug = jnp.concatenate([v_ref[...], jnp.ones((seq, HEAD_DIM), BF16)], axis=1)
    for qi in reversed(range(seq // tq)):
        q0, q1 = qi * tq, (qi + 1) * tq
        q = q_ref[q0:q1, :]
        diag = jnp.where(keep, _dot_nt(q, k_ref[q0:q1, :]), -jnp.inf)
        m = jnp.max(diag, axis=-1, keepdims=True)
        if qi:
            past = _dot_nt(q, k_ref[:q0, :])
            m = jnp.maximum(m, jnp.max(past, axis=-1, keepdims=True))
        acc = _dot(jnp.exp2((diag - m) * to_log2).astype(BF16), v_aug[q0:q1, :])
        if qi:
            acc = acc + _dot(jnp.exp2((past - m) * to_log2).astype(BF16), v_aug[:q0, :])
        o_ref[q0:q1, :] = (acc[:, :HEAD_DIM] / acc[:, HEAD_DIM:]).astype(BF16)


def _attention(q_aug, k_aug, v, batch, seq, tq, casts):
    m = v.shape[0]
    head_block = lambda b, hd: (b, hd)
    (o_b,), cast = _hosted_call(
        functools.partial(_attn_kernel, tq=tq),
        grid=(batch, HEADS),
        in_specs=[
            pl.BlockSpec((seq, AUG), head_block),
            pl.BlockSpec((seq, AUG), head_block),
            pl.BlockSpec((seq, HEAD_DIM), head_block),
        ],
        out_specs=[pl.BlockSpec((seq, HEAD_DIM), head_block)],
        out_shape=[jax.ShapeDtypeStruct((m, ATTN_WIDTH), BF16)],
        semantics=("parallel", "parallel"),
        name="attention",
        casts=casts,
    )(q_aug, k_aug, v)
    return o_b, cast


def _merge_kernel(h_ref, oa_ref, ob_ref, wga_ref, wgb_ref, wba_ref, wbb_ref, y_ref):
    h = h_ref[...]
    ya = _sigmoid(_dot_nt(h, wga_ref[...])) * _dot(oa_ref[...], wba_ref[...])
    yb = _sigmoid(_dot_nt(h, wgb_ref[...])) * _dot(ob_ref[...], wbb_ref[...])
    y_ref[...] = (ya + yb).astype(BF16)


def _merge(h, o_a, o_b, w_g_t, w_ba, w_bb, tm, tn, casts):
    m = h.shape[0]
    nn = D_MODEL // tn
    row = lambda i, j: (i, 0)
    col = lambda i, j: (0, j)
    (y,), cast = _hosted_call(
        _merge_kernel,
        grid=(m // tm, nn),
        in_specs=[
            pl.BlockSpec((tm, D_MODEL), row),
            pl.BlockSpec((tm, GMLP_WIDTH), row),
            pl.BlockSpec((tm, ATTN_WIDTH), row),
            pl.BlockSpec((tn, D_MODEL), lambda i, j: (j, 0)),
            pl.BlockSpec((tn, D_MODEL), lambda i, j: (nn + j, 0)),
            pl.BlockSpec((GMLP_WIDTH, tn), col),
            pl.BlockSpec((ATTN_WIDTH, tn), col),
        ],
        out_specs=[pl.BlockSpec((tm, tn), lambda i, j: (i, j))],
        out_shape=[jax.ShapeDtypeStruct((m, D_MODEL), BF16)],
        semantics=("parallel", "arbitrary"),
        name="merge",
        casts=casts,
    )(h, o_a, o_b, w_g_t, w_g_t, w_ba, w_bb)
    return y, cast


def _outproj_kernel(x_ref, y_ref, w_ref, g_ref, x1_ref, h2_ref):
    def project(r0):
        return _dot(y_ref[r0:r0 + SUB_ROWS, :], w_ref[...])

    def finish(r0, z):
        sub = slice(r0, r0 + SUB_ROWS)
        x1 = x_ref[sub, :] + z
        x1_ref[sub, :] = x1
        h2_ref[sub, :] = _rms(x1, g_ref[...]).astype(BF16)

    _software_pipeline(range(0, x_ref.shape[0], SUB_ROWS), project, finish)


def _outproj(x, y, w_out, g_next, tm):
    m = x.shape[0]
    row = lambda i: (i, 0)
    return pl.pallas_call(
        _outproj_kernel,
        grid=(m // tm,),
        in_specs=[
            pl.BlockSpec((tm, D_MODEL), row),
            pl.BlockSpec((tm, D_MODEL), row),
            _resident((D_MODEL, D_MODEL)),
            _resident((1, D_MODEL)),
        ],
        out_specs=[pl.BlockSpec((tm, D_MODEL), row), pl.BlockSpec((tm, D_MODEL), row)],
        out_shape=[jax.ShapeDtypeStruct((m, D_MODEL), F32), jax.ShapeDtypeStruct((m, D_MODEL), BF16)],
        compiler_params=_params("parallel"),
        name="outproj",
    )(x, y, w_out, g_next)


def _ffn_up_kernel(h_ref, wa_ref, wb_ref, cw_ref, cb_ref, o_ref):
    seq = h_ref.shape[0]

    def proj(w):
        return jnp.concatenate([_dot(h_ref[r:r + DOT_ROWS, :], w) for r in range(0, seq, DOT_ROWS)], axis=0)

    a = proj(wa_ref[...])
    t = lax.broadcasted_iota(jnp.int32, a.shape, 0)
    a1 = jnp.where(t >= 1, pltpu.roll(a, 1, 0), 0.0)
    a2 = jnp.where(t >= 2, pltpu.roll(a, 2, 0), 0.0)
    conv = cb_ref[...] + a2 * cw_ref[0:1, :] + a1 * cw_ref[1:2, :] + a * cw_ref[2:3, :]
    o_ref[...] = (_gelu_tanh(conv) * proj(wb_ref[...])).astype(BF16)


def _ffn_up(h2, w_up_a, w_up_b, conv_w, conv_b, seq, tf, casts):
    m = h2.shape[0]
    col = lambda b, j: (0, j)
    (hidden,), cast = _hosted_call(
        _ffn_up_kernel,
        grid=(m // seq, D_FF // tf),
        in_specs=[
            pl.BlockSpec((seq, D_MODEL), lambda b, j: (b, 0)),
            pl.BlockSpec((D_MODEL, tf), col),
            pl.BlockSpec((D_MODEL, tf), col),
            pl.BlockSpec((conv_w.shape[0], tf), col),
            pl.BlockSpec((1, tf), col),
        ],
        out_specs=[pl.BlockSpec((seq, tf), lambda b, j: (b, j))],
        out_shape=[jax.ShapeDtypeStruct((m, D_FF), BF16)],
        semantics=("parallel", "arbitrary"),
        name="ffn_up",
        casts=casts,
    )(h2, w_up_a, w_up_b, conv_w, conv_b)
    return hidden, cast


def _ffn_down_kernel(x_ref, hid_ref, w_ref, x2_ref):
    x2_ref[...] = x_ref[...] + _dot(hid_ref[...], w_ref[...])


def _ffn_down(x1, hidden, w_down, tm, casts):
    m = x1.shape[0]
    row = lambda i: (i, 0)
    (x2,), cast = _hosted_call(
        _ffn_down_kernel,
        grid=(m // tm,),
        in_specs=[
            pl.BlockSpec((tm, D_MODEL), row),
            pl.BlockSpec((tm, D_FF), row),
            _resident((D_FF, D_MODEL)),
        ],
        out_specs=[pl.BlockSpec((tm, D_MODEL), row)],
        out_shape=[jax.ShapeDtypeStruct((m, D_MODEL), F32)],
        semantics=("parallel",),
        name="ffn_down",
        casts=casts,
    )(x1, hidden, w_down)
    return x2, cast


def _ple_kernel(x_ref, p_ref, gn_ref, wg_ref, wp_ref, g_ref, o_ref):
    w_proj = wp_ref[...].astype(BF16)

    def project(r0):
        sub = slice(r0, r0 + SUB_ROWS)
        e = _dot(p_ref[sub, :].astype(BF16), w_proj)
        return e, _dot(_rms(x_ref[sub, :], gn_ref[...]).astype(BF16), wg_ref[...])

    def finish(r0, z):
        sub = slice(r0, r0 + SUB_ROWS)
        o_ref[sub, :] = x_ref[sub, :] + _sigmoid(z[1]) * _rms(z[0], g_ref[...])

    _software_pipeline(range(0, x_ref.shape[0], SUB_ROWS), project, finish)


def _ple(x2, p, g_gate_norm, w_gate, w_proj, g, tm):
    m = x2.shape[0]
    row = lambda i: (i, 0)
    return pl.pallas_call(
        _ple_kernel,
        grid=(m // tm,),
        in_specs=[
            pl.BlockSpec((tm, D_MODEL), row),
            pl.BlockSpec((tm, PLE_DIM), row),
            _resident((1, D_MODEL)),
            _resident((D_MODEL, D_MODEL)),
            _resident((PLE_DIM, D_MODEL)),
            _resident((1, D_MODEL)),
        ],
        out_specs=pl.BlockSpec((tm, D_MODEL), row),
        out_shape=jax.ShapeDtypeStruct((m, D_MODEL), F32),
        compiler_params=_params("parallel"),
        name="ple",
    )(x2, p, g_gate_norm, w_gate, w_proj, g)


def _layer(x, p, norm_mix_g, w_in, gmlp_ln_g, gmlp_ln_b, gmlp_w_s, gmlp_b_s, fox_b_f, q_norm_g,
           k_norm_g, w_branch_a, w_branch_b, w_out, norm_ffn_g, w_up, conv_w, conv_b, w_down,
           ple_proj, ple_norm_g, ple_gate_norm_g, w_ple_gate, *, batch, seq):
    row_vec = lambda v: v.reshape(1, -1)
    whole = lambda w: CastJob(w, 0, w.shape[0], 0, w.shape[1])
    w_in_t = w_in.T
    b_f = jnp.pad(jnp.tile(fox_b_f, N_SPLIT), (0, LANES - N_SPLIT * HEADS)).reshape(1, LANES)

    o_a, h, (w_qkv_t,) = _gmlp(
        x, row_vec(norm_mix_g), w_in_t, row_vec(gmlp_ln_g), row_vec(gmlp_ln_b), gmlp_w_s, gmlp_b_s.T,
        tm=TILES.gmlp_rows, casts=[CastJob(w_in_t, Q_COLS, 3 * ATTN_WIDTH, 0, D_MODEL)])
    q_aug, k_aug, v, (w_g_t, w_ba_bf, w_bb_bf) = _qkv(
        h, w_qkv_t, w_in_t, b_f, row_vec(q_norm_g), row_vec(k_norm_g), tm=TILES.qkv_rows, seq=seq,
        casts=[CastJob(w_in_t, G_COLS, 2 * D_MODEL, 0, D_MODEL), whole(w_branch_a), whole(w_branch_b)])
    o_b, (w_up_a,) = _attention(
        q_aug, k_aug, v, batch, seq, tq=TILES.attn_query_rows, casts=[CastJob(w_up, 0, D_MODEL, 0, D_FF)])
    y, (w_out_bf, w_up_b) = _merge(
        h, o_a, o_b, w_g_t, w_ba_bf, w_bb_bf, tm=TILES.merge_rows, tn=TILES.merge_cols,
        casts=[whole(w_out), CastJob(w_up, 0, D_MODEL, D_FF, D_FF)])
    x1, h2 = _outproj(x, y, w_out_bf, row_vec(norm_ffn_g), tm=TILES.outproj_rows)
    hidden, (w_down_bf,) = _ffn_up(h2, w_up_a, w_up_b, conv_w, row_vec(conv_b), seq=seq, tf=TILES.ffn_cols,
                                   casts=[whole(w_down)])
    x2, (w_gate_bf,) = _ffn_down(x1, hidden, w_down_bf, tm=TILES.ffn_down_rows, casts=[whole(w_ple_gate)])
    return _ple(x2, p, row_vec(ple_gate_norm_g), w_gate_bf, ple_proj, row_vec(ple_norm_g), tm=TILES.ple_rows)


def kernel(x, p, norm_mix_g, w_in, gmlp_ln_g, gmlp_ln_b, gmlp_w_s, gmlp_b_s, fox_b_f, q_norm_g, k_norm_g, w_branch_a, w_branch_b, w_out, norm_ffn_g, w_up, conv_w, conv_b, w_down, ple_proj, ple_norm_g, ple_gate_norm_g, w_ple_gate):
    batch, seq, d = x.shape
    depth = w_in.shape[0]
    xs = x.reshape(batch * seq, d)
    for i in range(depth):
        xs = _layer(xs, p[i].reshape(batch * seq, -1), norm_mix_g[i], w_in[i], gmlp_ln_g[i], gmlp_ln_b[i],
                    gmlp_w_s[i], gmlp_b_s[i], fox_b_f[i], q_norm_g[i], k_norm_g[i], w_branch_a[i],
                    w_branch_b[i], w_out[i], norm_ffn_g[i], w_up[i], conv_w[i], conv_b[i], w_down[i],
                    ple_proj[i], ple_norm_g[i], ple_gate_norm_g[i], w_ple_gate[i], batch=batch, seq=seq)
    return xs.reshape(batch, seq, d)
```
